```python
import math
import jax, jax.numpy as jnp
from jax import lax
import numpy as np

D_MODEL = 1024
BATCH = 8
SEQ = 4096
DEPTH = 2

N_EVEN = (DEPTH + 1) // 2
N_ODD = DEPTH // 2

HGRN_DIM = D_MODEL // 2
HGRN_HEAD_DIM = 128
HGRN_HEADS = HGRN_DIM // HGRN_HEAD_DIM
HGRN_CHUNK = 64

S5_DIM = D_MODEL - HGRN_DIM
S5_GROUP = 16
S5_GROUPS = S5_DIM // S5_GROUP
S5_STATE = 64
S5_DT_MIN = 1e-3
S5_DT_MAX = 1e-1

EVEN_IN = 4 * HGRN_DIM + S5_DIM

MLA_HEADS = 8
MLA_Q_RANK = 384
MLA_KV_RANK = 256
MLA_NOPE = 128
MLA_ROPE = 64
MLA_V = 128
MLA_QK = MLA_NOPE + MLA_ROPE
ODD_IN = MLA_Q_RANK + MLA_KV_RANK + MLA_ROPE
ROPE_THETA = 10000.0
Q_BLOCK = 128

D_FF = 2816
CONV_W = 3
EPS = 1e-6

kernel_name = "hybrid_hgrn2_s5_mla_convffn"


def rmsnorm(x, g):
    xf = x.astype(jnp.float32)
    y = xf * lax.rsqrt(jnp.mean(xf * xf, axis=-1, keepdims=True) + EPS)
    return (y * g.astype(jnp.float32)).astype(x.dtype)


def hgrn2(q, f, i, g, lb, norm_g):
    f32 = jnp.float32
    B_, S_, _ = q.shape
    H, Dh, C = HGRN_HEADS, HGRN_HEAD_DIM, HGRN_CHUNK
    N = S_ // C
    lb = lb.astype(f32)
    forget = lb + (1.0 - lb) * jax.nn.sigmoid(f.astype(f32))
    key_in = 1.0 - forget

    def heads(t):
        return t.astype(f32).reshape(B_, N, C, H, Dh).transpose(0, 3, 1, 2, 4)

    qh, kh, vh = heads(q), heads(key_in), heads(i)
    b = jnp.cumsum(heads(jnp.log(forget)), axis=3)
    b_last = b[:, :, :, -1:, :]
    qd = qh * jnp.exp(b)
    kd = kh * jnp.exp(-b)
    causal = jnp.tril(jnp.ones((C, C), dtype=bool))
    att = jnp.where(causal, jnp.einsum('bhntd,bhnsd->bhnts', qd, kd), 0.0)
    o_intra = jnp.einsum('bhnts,bhnsv->bhntv', att, vh)
    d_state = jnp.einsum('bhnsd,bhnsv->bhndv', kh * jnp.exp(b_last - b), vh)
    decay = jnp.exp(b_last[:, :, :, 0, :])

    def step(state, inp):
        ds_n, dec_n = inp
        return dec_n[..., None] * state + ds_n, state

    s0 = jnp.zeros((B_, H, Dh, Dh), f32)
    _, s_start = lax.scan(step, s0, (d_state.transpose(2, 0, 1, 3, 4), decay.transpose(2, 0, 1, 3)))
    s_start = s_start.transpose(1, 2, 0, 3, 4)
    o = o_intra + jnp.einsum('bhntd,bhndv->bhntv', qd, s_start)
    o = o.transpose(0, 2, 3, 1, 4).reshape(B_, S_, H, Dh)
    o = o * lax.rsqrt(jnp.mean(o * o, axis=-1, keepdims=True) + EPS) * norm_g.astype(f32).reshape(H, Dh)
    return o.reshape(B_, S_, HGRN_DIM) * jax.nn.silu(g.astype(f32))


def s5(u, a_re, a_im, log_dt, b_re, b_im, c_re, c_im, d_skip, w_glu, b_glu):
    f32 = jnp.float32
    B_, S_, _ = u.shape
    G, P, Hc = S5_GROUPS, S5_STATE, S5_GROUP
    uf = u.astype(f32).reshape(B_, S_, G, Hc)
    ar, ai = a_re.astype(f32), a_im.astype(f32)
    dt = jnp.exp(log_dt.astype(f32))[:, None]
    mag = jnp.exp(ar * dt)
    abar_re, abar_im = mag * jnp.cos(ai * dt), mag * jnp.sin(ai * dt)
    den = ar * ar + ai * ai
    xr, xi = abar_re - 1.0, abar_im
    coef_re = ((xr * ar + xi * ai) / den)[..., None]
    coef_im = ((xi * ar - xr * ai) / den)[..., None]
    br, bi = b_re.astype(f32), b_im.astype(f32)
    bb_re = coef_re * br - coef_im * bi
    bb_im = coef_re * bi + coef_im * br
    bu_re = jnp.einsum('bsgh,gph->bsgp', uf, bb_re)
    bu_im = jnp.einsum('bsgh,gph->bsgp', uf, bb_im)
    a_seq_re = jnp.broadcast_to(abar_re, (1, S_, G, P))
    a_seq_im = jnp.broadcast_to(abar_im, (1, S_, G, P))

    def combine(left, right):
        a1r, a1i, b1r, b1i = left
        a2r, a2i, b2r, b2i = right
        return (a2r * a1r - a2i * a1i, a2r * a1i + a2i * a1r,
                a2r * b1r - a2i * b1i + b2r, a2r * b1i + a2i * b1r + b2i)

    _, _, s_re, s_im = lax.associative_scan(combine, (a_seq_re, a_seq_im, bu_re, bu_im), axis=1)
    y = (jnp.einsum('bsgp,ghp->bsgh', s_re, c_re.astype(f32))
         - jnp.einsum('bsgp,ghp->bsgh', s_im, c_im.astype(f32))
         + d_skip.astype(f32).reshape(G, Hc) * uf)
    z = jax.nn.gelu(y.reshape(B_, S_, S5_DIM))
    return z * jax.nn.sigmoid(z @ w_glu.astype(f32) + b_glu.astype(f32))


def even_mixer(hn, w_in, lb, hgrn_norm_g, a_re, a_im, log_dt, b_re, b_im, c_re, c_im,
               d_skip, w_glu, b_glu, w_out):
    proj = hn @ w_in
    q, f, i, g, u = jnp.split(proj, [HGRN_DIM, 2 * HGRN_DIM, 3 * HGRN_DIM, 4 * HGRN_DIM], axis=-1)
    y_a = hgrn2(q, f, i, g, lb, hgrn_norm_g).astype(hn.dtype)
    y_b = s5(u, a_re, a_im, log_dt, b_re, b_im, c_re, c_im, d_skip, w_glu, b_glu).astype(hn.dtype)
    return jnp.concatenate([y_a, y_b], axis=-1) @ w_out


def rotate(x, cos, sin):
    x1, x2 = jnp.split(x, 2, axis=-1)
    return jnp.concatenate([x1 * cos - x2 * sin, x1 * sin + x2 * cos], axis=-1)


def mla(hn, positions, w_in, q_norm_g, w_uq, kv_norm_g, w_ukv, w_out):
    B_, S_, _ = hn.shape
    H = MLA_HEADS
    proj = hn @ w_in
    cq, ckv, k_rope = jnp.split(proj, [MLA_Q_RANK, MLA_Q_RANK + MLA_KV_RANK], axis=-1)
    q = (rmsnorm(cq, q_norm_g) @ w_uq).reshape(B_, S_, H, MLA_QK)
    kv = (rmsnorm(ckv, kv_norm_g) @ w_ukv).reshape(B_, S_, H, MLA_NOPE + MLA_V)
    q_nope, q_rope = jnp.split(q, [MLA_NOPE], axis=-1)
    k_nope, v = jnp.split(kv, [MLA_NOPE], axis=-1)
    freqs = ROPE_THETA ** (-jnp.arange(0, MLA_ROPE, 2, dtype=jnp.float32) / MLA_ROPE)
    ang = positions.astype(jnp.float32)[..., None] * freqs
    cos = jnp.cos(ang)[:, :, None, :].astype(hn.dtype)
    sin = jnp.sin(ang)[:, :, None, :].astype(hn.dtype)
    q_rope = rotate(q_rope, cos, sin)
    k_rope = rotate(k_rope[:, :, None, :], cos, sin)
    q = jnp.concatenate([q_nope, q_rope], axis=-1)
    k = jnp.concatenate([k_nope, jnp.broadcast_to(k_rope, (B_, S_, H, MLA_ROPE))], axis=-1)
    scale = MLA_QK ** -0.5
    nb = S_ // Q_BLOCK
    qb = q.reshape(B_, nb, Q_BLOCK, H, MLA_QK).transpose(1, 0, 2, 3, 4)
    kpos = jnp.arange(S_)

    def attend(args):
        q_blk, blk = args
        s = jnp.einsum('bqhd,bkhd->bhqk', q_blk, k).astype(jnp.float32) * scale
        qpos = blk * Q_BLOCK + jnp.arange(Q_BLOCK)
        s = jnp.where(kpos[None, :] <= qpos[:, None], s, -jnp.inf)
        p = jax.nn.softmax(s, axis=-1).astype(v.dtype)
        return jnp.einsum('bhqk,bkhd->bqhd', p, v)

    o = lax.map(attend, (qb, jnp.arange(nb)))
    o = o.transpose(1, 0, 2, 3, 4).reshape(B_, S_, H * MLA_V)
    return o @ w_out


def conv_ffn(hn, w_in, conv_w, conv_b, w_out):
    a, u = jnp.split(hn @ w_in, 2, axis=-1)
    a = lax.conv_general_dilated(a, conv_w[:, None, :], window_strides=(1,),
                                 padding=[(CONV_W - 1, 0)],
                                 dimension_numbers=('NWC', 'WIO', 'NWC'),
                                 feature_group_count=D_FF) + conv_b
    return (jax.nn.silu(a) * u) @ w_out


def setup_inputs(seed: int = 0) -> dict:
    key = jax.random.key(seed)
    ks = iter(jax.random.split(key, 40))
    nrm = lambda shape, s: jax.random.normal(next(ks), shape, jnp.float32) * s
    D = D_MODEL
    G, P, Hc = S5_GROUPS, S5_STATE, S5_GROUP
    x = nrm((BATCH, SEQ, D), 1.0)
    offset = jax.random.randint(next(ks), (BATCH, 1), 0, SEQ)
    positions = (offset + jnp.arange(SEQ)[None, :]).astype(jnp.int32)
    return {
        "x": x,
        "positions": positions,
        "norm_mix_g": 1.0 + nrm((DEPTH, D), 0.02),
        "norm_ffn_g": 1.0 + nrm((DEPTH, D), 0.02),
        "final_norm_g": 1.0 + nrm((D,), 0.02),
        "even_w_in": nrm((N_EVEN, D, EVEN_IN), D ** -0.5),
        "hgrn_lb_logits": nrm((N_EVEN + 1, HGRN_DIM), 0.1),
        "hgrn_norm_g": 1.0 + nrm((N_EVEN, HGRN_DIM), 0.02),
        "s5_a_re": -0.5 + nrm((N_EVEN, G, P), 0.01),
        "s5_a_im": jnp.pi * jnp.arange(P, dtype=jnp.float32) + nrm((N_EVEN, G, P), 0.01),
        "s5_log_dt": jax.random.uniform(next(ks), (N_EVEN, G), jnp.float32,
                                        math.log(S5_DT_MIN), math.log(S5_DT_MAX)),
        "s5_b_re": nrm((N_EVEN, G, P, Hc), (2 * Hc) ** -0.5),
        "s5_b_im": nrm((N_EVEN, G, P, Hc), (2 * Hc) ** -0.5),
        "s5_c_re": nrm((N_EVEN, G, Hc, P), (2 * P) ** -0.5),
        "s5_c_im": nrm((N_EVEN, G, Hc, P), (2 * P) ** -0.5),
        "s5_d": nrm((N_EVEN, S5_DIM), 1.0),
        "s5_w_glu": nrm((N_EVEN, S5_DIM, S5_DIM), S5_DIM ** -0.5),
        "s5_b_glu": nrm((N_EVEN, S5_DIM), 0.02),
        "even_w_out": nrm((N_EVEN, HGRN_DIM + S5_DIM, D), (HGRN_DIM + S5_DIM) ** -0.5),
        "odd_w_in": nrm((N_ODD, D, ODD_IN), D ** -0.5),
        "mla_q_norm_g": 1.0 + nrm((N_ODD, MLA_Q_RANK), 0.02),
        "mla_w_uq": nrm((N_ODD, MLA_Q_RANK, MLA_HEADS * MLA_QK), MLA_Q_RANK ** -0.5),
        "mla_kv_norm_g": 1.0 + nrm((N_ODD, MLA_KV_RANK), 0.02),
        "mla_w_ukv": nrm((N_ODD, MLA_KV_RANK, MLA_HEADS * (MLA_NOPE + MLA_V)), MLA_KV_RANK ** -0.5),
        "odd_w_out": nrm((N_ODD, MLA_HEADS * MLA_V, D), (MLA_HEADS * MLA_V) ** -0.5),
        "ffn_w_in": nrm((DEPTH, D, 2 * D_FF), D ** -0.5),
        "ffn_conv_w": nrm((DEPTH, CONV_W, D_FF), CONV_W ** -0.5),
        "ffn_conv_b": nrm((DEPTH, D_FF), 0.02),
        "ffn_w_out": nrm((DEPTH, D_FF, D), D_FF ** -0.5),
    }


def reference(x, positions, norm_mix_g, norm_ffn_g, final_norm_g,
              even_w_in, hgrn_lb_logits, hgrn_norm_g,
              s5_a_re, s5_a_im, s5_log_dt, s5_b_re, s5_b_im, s5_c_re, s5_c_im,
              s5_d, s5_w_glu, s5_b_glu, even_w_out,
              odd_w_in, mla_q_norm_g, mla_w_uq, mla_kv_norm_g, mla_w_ukv, odd_w_out,
              ffn_w_in, ffn_conv_w, ffn_conv_b, ffn_w_out):
    lower_bounds = jnp.cumsum(jax.nn.softmax(hgrn_lb_logits.astype(jnp.float32), axis=0), axis=0)
    h = x
    for layer in range(DEPTH):
        j = layer // 2
        hn = rmsnorm(h, norm_mix_g[layer])
        if layer % 2 == 0:
            mix = even_mixer(hn, even_w_in[j], lower_bounds[j], hgrn_norm_g[j],
                             s5_a_re[j], s5_a_im[j], s5_log_dt[j], s5_b_re[j], s5_b_im[j],
                             s5_c_re[j], s5_c_im[j], s5_d[j], s5_w_glu[j], s5_b_glu[j],
                             even_w_out[j])
        else:
            mix = mla(hn, positions, odd_w_in[j], mla_q_norm_g[j], mla_w_uq[j],
                      mla_kv_norm_g[j], mla_w_ukv[j], odd_w_out[j])
        h = h + mix
        h = h + conv_ffn(rmsnorm(h, norm_ffn_g[layer]), ffn_w_in[layer], ffn_conv_w[layer],
                         ffn_conv_b[layer], ffn_w_out[layer])
    return rmsnorm(h, final_norm_g)
```

```python
import functools
import math

import jax
import jax.numpy as jnp
from jax import lax
from jax.experimental import pallas as pl
from jax.experimental.pallas import tpu as pltpu

F32 = jnp.float32
BF16 = jnp.bfloat16

EPS = 1e-6
HGRN_HEADS = 4
HGRN_HEAD_DIM = 128
HGRN_CHUNK = 64
S5_GROUP = 16
S5_STATE = 64
S5_BLOCK_GROUPS = 8
S5_CHUNK = 64
MLA_HEADS = 8
MLA_Q_RANK = 384
MLA_KV_RANK = 256
MLA_NOPE = 128
MLA_ROPE = 64
MLA_V = 128
ROPE_THETA = 10000.0
LANES = 128
MLA_QPAD = 2 * LANES

VMEM_LIMIT = 56 * 1024 * 1024


def _params(n_axes, vmem=VMEM_LIMIT):
    return pltpu.CompilerParams(dimension_semantics=("arbitrary",) * n_axes,
                                vmem_limit_bytes=vmem)


def _const_spec(shape):
    zeros = (0,) * len(shape)
    return pl.BlockSpec(shape, lambda *_: zeros, pipeline_mode=pl.Buffered(1))


def _rms(x, g):
    return x * lax.rsqrt(jnp.mean(x * x, axis=-1, keepdims=True) + EPS) * g


def _sigmoid(x):
    return 1.0 / (1.0 + jnp.exp(-x))


def _dot(a, b):
    return jnp.dot(a, b, preferred_element_type=F32)


def _dot_nt(a, b):
    return lax.dot_general(a, b, (((1,), (1,)), ((), ())), preferred_element_type=F32)


def _dot_tn(a, b):
    return lax.dot_general(a, b, (((0,), (0,)), ((), ())), preferred_element_type=F32)


def _norm_proj_kernel(h_ref, g_ref, w_ref, o_ref):
    hn = _rms(h_ref[...], g_ref[...]).astype(BF16)
    o_ref[...] = _dot(hn, w_ref[...]).astype(o_ref.dtype)


def _norm_proj(h, g, w, tm=512):
    t, d = h.shape
    n = w.shape[1]
    return pl.pallas_call(
        _norm_proj_kernel,
        grid=(t // tm,),
        in_specs=[pl.BlockSpec((tm, d), lambda i: (i, 0)),
                  _const_spec((1, d)),
                  _const_spec((d, n))],
        out_specs=pl.BlockSpec((tm, n), lambda i: (i, 0)),
        out_shape=jax.ShapeDtypeStruct((t, n), F32),
        compiler_params=_params(1),
        name="even_in",
    )(h, g.reshape(1, d), w)


HGRN_TILE = 512
HGRN_ATT_BLOCK = 256


def _hgrn_kernel(q_ref, f_ref, i_ref, g_ref, lb_ref, ng_ref, o_ref, st_ref):
    L, C, AB = HGRN_TILE, HGRN_CHUNK, HGRN_ATT_BLOCK
    dh = HGRN_HEAD_DIM

    @pl.when(pl.program_id(2) == 0)
    def _():
        st_ref[...] = jnp.zeros_like(st_ref)

    q = q_ref[0]
    f = f_ref[0]
    v = i_ref[0]
    g = g_ref[0]
    lb = lb_ref[...]
    forget = lb + (1.0 - lb) * _sigmoid(f)
    kh = 1.0 - forget
    rc = lax.broadcasted_iota(jnp.int32, (L, dh), 0) & (C - 1)
    b = jnp.log(forget)
    s = 1
    while s < C:
        b = b + jnp.where(rc >= s, pltpu.roll(b, s, axis=0), 0.0)
        s *= 2
    qd = (q * jnp.exp(b)).astype(BF16)
    kd = (kh * jnp.exp(-b)).astype(BF16)
    vb = v.astype(BF16)

    ri = lax.broadcasted_iota(jnp.int32, (AB, AB), 0)
    ci = lax.broadcasted_iota(jnp.int32, (AB, AB), 1)
    mask = jnp.logical_and((ri // C) == (ci // C), ci <= ri)
    o_intra = []
    for blk in range(L // AB):
        r0 = blk * AB
        att = _dot_nt(qd[r0:r0 + AB], kd[r0:r0 + AB])
        att = jnp.where(mask, att, 0.0).astype(BF16)
        o_intra.append(_dot(att, vb[r0:r0 + AB]))

    st = st_ref[...]
    o_inter = []
    for c in range(L // C):
        r0 = c * C
        b_c = b[r0:r0 + C]
        bl = b_c[C - 1:C, :]
        kdec = (kh[r0:r0 + C] * jnp.exp(bl - b_c)).astype(BF16)
        o_inter.append(_dot_nt(qd[r0:r0 + C], st.astype(BF16)))
        st = st * jnp.exp(bl) + _dot_tn(vb[r0:r0 + C], kdec)
    st_ref[...] = st

    o = jnp.concatenate(o_intra, axis=0) + jnp.concatenate(o_inter, axis=0)
    o = _rms(o, ng_ref[...])
    o_ref[0] = (o * (g * _sigmoid(g))).astype(o_ref.dtype)


def _hgrn(proj3, lb, norm_g):
    bsz, seq, _ = proj3.shape
    nh, dh, L = HGRN_HEADS, HGRN_HEAD_DIM, HGRN_TILE

    def col(k):
        return pl.BlockSpec((1, L, dh), lambda b, h, s, k=k: (b, s, k * nh + h))

    vec = pl.BlockSpec((1, dh), lambda b, h, s: (0, h))
    return pl.pallas_call(
        _hgrn_kernel,
        grid=(bsz, nh, seq // L),
        in_specs=[col(0), col(1), col(2), col(3), vec, vec],
        out_specs=pl.BlockSpec((1, L, dh), lambda b, h, s: (b, s, h)),
        out_shape=jax.ShapeDtypeStruct((bsz, seq, nh * dh), BF16),
        scratch_shapes=[pltpu.VMEM((dh, dh), F32)],
        compiler_params=_params(3),
        name="hgrn2",
    )(proj3, proj3, proj3, proj3, lb.reshape(1, nh * dh), norm_g.reshape(1, nh * dh))


S5_TILE = 256


def _s5_tables(a_re, a_im, log_dt, b_re, b_im, c_re, c_im):
    G, P = a_re.shape
    Hc = b_re.shape[-1]
    GB = S5_BLOCK_GROUPS
    NB = G // GB
    Lc = S5_CHUNK
    dt = jnp.exp(log_dt)[:, None]
    lam, th = a_re * dt, a_im * dt
    mag = jnp.exp(lam)
    abar_re, abar_im = mag * jnp.cos(th), mag * jnp.sin(th)
    den = a_re * a_re + a_im * a_im
    xr, xi = abar_re - 1.0, abar_im
    coef_re = ((xr * a_re + xi * a_im) / den)[..., None]
    coef_im = ((xi * a_re - xr * a_im) / den)[..., None]
    bb_re = coef_re * b_re - coef_im * b_im
    bb_im = coef_re * b_im + coef_im * b_re
    eye = jnp.eye(GB, dtype=F32)

    def blk_in(bb):
        return jnp.einsum('kgph,gj->kghjp', bb.reshape(NB, GB, P, Hc), eye).reshape(NB, GB * Hc, GB * P)

    def blk_out(c):
        return jnp.einsum('kghp,gj->kjpgh', c.reshape(NB, GB, Hc, P), eye).reshape(NB, GB * P, GB * Hc)

    bmat = jnp.concatenate([blk_in(bb_re), blk_in(bb_im)], axis=-1).astype(BF16)
    cmat = jnp.concatenate([blk_out(c_re), -blk_out(c_im)], axis=1).astype(BF16)

    def lay(re, im):
        L = re.shape[0]
        return jnp.concatenate([re.reshape(L, NB, GB * P), im.reshape(L, NB, GB * P)],
                               axis=-1).reshape(L, NB * 2 * GB * P)

    t = jnp.arange(Lc, dtype=F32)[:, None, None]
    pm, im = jnp.exp(t * lam), jnp.exp(-t * lam)
    wp = lay(pm * jnp.cos(t * th), pm * jnp.sin(t * th))
    winv = lay(im * jnp.cos(t * th), -im * jnp.sin(t * th))
    abar1 = lay(abar_re[None], abar_im[None])
    return bmat, cmat, winv, wp, abar1


def _gelu_tanh(x):
    return 0.5 * x * (1.0 + jnp.tanh(math.sqrt(2.0 / math.pi) * (x + 0.044715 * (x * x * x))))


def _s5_kernel(u_ref, bmat_ref, cmat_ref, winv_ref, wp_ref, abar_ref, tri_ref, d_ref,
               wglu_ref, bglu_ref, o_ref, hst_ref):
    TS, Lc = S5_TILE, S5_CHUNK
    nb = bmat_ref.shape[0]
    gw = bmat_ref.shape[1]
    half = bmat_ref.shape[2] // 2

    @pl.when(pl.program_id(1) == 0)
    def _():
        hst_ref[...] = jnp.zeros_like(hst_ref)

    u = u_ref[0]
    ub = u.astype(BF16)
    tri = tri_ref[...]
    ys = []
    for k in range(nb):
        lo = k * 2 * half
        re = slice(lo, lo + half)
        im = slice(lo + half, lo + 2 * half)
        bu = _dot(ub[:, k * gw:(k + 1) * gw], bmat_ref[k])
        wir, wii = winv_ref[:, re], winv_ref[:, im]
        wpr, wpi = wp_ref[:, re], wp_ref[:, im]
        a1r, a1i = abar_ref[:, re], abar_ref[:, im]
        hr_prev, hi_prev = hst_ref[:, re], hst_ref[:, im]
        hs = []
        for c in range(TS // Lc):
            rows = slice(c * Lc, (c + 1) * Lc)
            bur, bui = bu[rows, :half], bu[rows, half:]
            x = jnp.concatenate([bur * wir - bui * wii, bur * wii + bui * wir], axis=1).astype(BF16)
            cs = _dot(tri, x)
            cr = cs[:, :half] + (a1r * hr_prev - a1i * hi_prev)
            cim = cs[:, half:] + (a1r * hi_prev + a1i * hr_prev)
            hr = cr * wpr - cim * wpi
            hi = cr * wpi + cim * wpr
            hr_prev, hi_prev = hr[Lc - 1:Lc, :], hi[Lc - 1:Lc, :]
            hs.append(jnp.concatenate([hr, hi], axis=1).astype(BF16))
        hst_ref[:, re] = hr_prev
        hst_ref[:, im] = hi_prev
        ys.append(_dot(jnp.concatenate(hs, axis=0), cmat_ref[k]))
    y = jnp.concatenate(ys, axis=1) + d_ref[...] * u
    z = _gelu_tanh(y)
    gate = _sigmoid(_dot(z.astype(BF16), wglu_ref[...]) + bglu_ref[...])
    o_ref[0] = (z * gate).astype(o_ref.dtype)


def _s5(proj3, u_col_block, tables, d_skip, w_glu, b_glu):
    bsz, seq, _ = proj3.shape
    bmat, cmat, winv, wp, abar1 = tables
    dim = d_skip.shape[0]
    TS, Lc = S5_TILE, S5_CHUNK
    tri = jnp.tril(jnp.ones((Lc, Lc), F32)).astype(BF16)
    nstate = winv.shape[1]
    return pl.pallas_call(
        _s5_kernel,
        grid=(bsz, seq // TS),
        in_specs=[pl.BlockSpec((1, TS, dim), lambda b, s: (b, s, u_col_block)),
                  _const_spec(bmat.shape), _const_spec(cmat.shape),
                  _const_spec(winv.shape), _const_spec(wp.shape), _const_spec(abar1.shape),
                  _const_spec(tri.shape), _const_spec((1, dim)),
                  _const_spec(w_glu.shape), _const_spec((1, dim))],
        out_specs=pl.BlockSpec((1, TS, dim), lambda b, s: (b, s, 0)),
        out_shape=jax.ShapeDtypeStruct((bsz, seq, dim), BF16),
        scratch_shapes=[pltpu.VMEM((1, nstate), F32)],
        compiler_params=_params(2),
        name="s5",
    )(proj3, bmat, cmat, winv, wp, abar1, tri, d_skip.reshape(1, dim),
      w_glu.astype(BF16), b_glu.reshape(1, dim))


def _even_out_kernel(ya_ref, yb_ref, wa_ref, wb_ref, h_ref, o_ref):
    o_ref[...] = h_ref[...] + _dot(ya_ref[...], wa_ref[...]) + _dot(yb_ref[...], wb_ref[...])


def _even_out(ya, yb, w_out, h, tm=1024):
    t, d = h.shape
    ka, kb = ya.shape[1], yb.shape[1]
    return pl.pallas_call(
        _even_out_kernel,
        grid=(t // tm,),
        in_specs=[pl.BlockSpec((tm, ka), lambda i: (i, 0)),
                  pl.BlockSpec((tm, kb), lambda i: (i, 0)),
                  _const_spec((ka, d)), _const_spec((kb, d)),
                  pl.BlockSpec((tm, d), lambda i: (i, 0))],
        out_specs=pl.BlockSpec((tm, d), lambda i: (i, 0)),
        out_shape=jax.ShapeDtypeStruct((t, d), F32),
        compiler_params=_params(1),
        name="even_out",
    )(ya, yb, w_out[:ka].astype(BF16), w_out[ka:].astype(BF16), h)


FFN_TILE = 512
FFN_CHUNK = 256
SUBLANES = 8


def _ffn_kernel(tiles_per_seq, final_norm, h_ref, g_ref, wa_ref, wu_ref, cw_ref, cb_ref,
                wo_ref, fg_ref, o_ref, prev_ref, abuf_ref, act_ref):
    tm, fc, hal = FFN_TILE, FFN_CHUNK, SUBLANES
    dff = wa_ref.shape[1]

    @pl.when(pl.program_id(0) % tiles_per_seq == 0)
    def _():
        prev_ref[...] = jnp.zeros_like(prev_ref)

    x = h_ref[...]
    hn = _rms(x, g_ref[...]).astype(BF16)
    for c in range(dff // fc):
        cols = slice(c * fc, (c + 1) * fc)
        a = _dot(hn, wa_ref[:, cols])
        u = _dot(hn, wu_ref[:, cols])
        ab = abuf_ref.at[c % 2]
        ab[0:hal, :] = prev_ref[:, cols]
        ab[hal:hal + tm, :] = a
        prev_ref[:, cols] = a[tm - hal:tm, :]
        a1 = ab[hal - 1:hal - 1 + tm, :]
        a2 = ab[hal - 2:hal - 2 + tm, :]
        w = cw_ref[:, cols]
        conv = a * w[2:3, :] + a1 * w[1:2, :] + a2 * w[0:1, :] + cb_ref[:, cols]
        act_ref[:, cols] = (conv * _sigmoid(conv) * u).astype(BF16)
    y = x + _dot(act_ref[...], wo_ref[...])
    if final_norm:
        y = _rms(y, fg_ref[...])
    o_ref[...] = y


def _ffn(h, seq, g, w_in, conv_w, conv_b, w_out, final_g=None):
    t, d = h.shape
    dff = w_out.shape[0]
    tm, fc, hal = FFN_TILE, FFN_CHUNK, SUBLANES
    final_norm = final_g is not None
    fg = final_g if final_norm else g
    w_in = w_in.astype(BF16)
    return pl.pallas_call(
        functools.partial(_ffn_kernel, seq // tm, final_norm),
        grid=(t // tm,),
        in_specs=[pl.BlockSpec((tm, d), lambda i: (i, 0)),
                  _const_spec((1, d)),
                  _const_spec((d, dff)), _const_spec((d, dff)),
                  _const_spec((conv_w.shape[0], dff)), _const_spec((1, dff)),
                  _const_spec((dff, d)), _const_spec((1, d))],
        out_specs=pl.BlockSpec((tm, d), lambda i: (i, 0)),
        out_shape=jax.ShapeDtypeStruct((t, d), F32),
        scratch_shapes=[pltpu.VMEM((hal, dff), F32),
                        pltpu.VMEM((2, hal + tm, fc), F32),
                        pltpu.VMEM((tm, dff), BF16)],
        compiler_params=_params(1),
        name="ffn",
    )(h, g.reshape(1, d), w_in[:, :dff], w_in[:, dff:], conv_w, conv_b.reshape(1, dff),
      w_out.astype(BF16), fg.reshape(1, d))


MLA_TILE = 512


def _mla_proj_kernel(h_ref, g_ref, win_ref, qg_ref, wuq_ref, kvg_ref, wukv_ref, cs_ref, sn_ref,
                     q_ref, kn_ref, kr_ref, v_ref):
    nh = MLA_HEADS
    hn = _rms(h_ref[...], g_ref[...]).astype(BF16)
    proj = _dot(hn, win_ref[...])
    cq = _rms(proj[:, :MLA_Q_RANK], qg_ref[...]).astype(BF16)
    ckv = _rms(proj[:, MLA_Q_RANK:MLA_Q_RANK + MLA_KV_RANK], kvg_ref[...]).astype(BF16)
    cs, sn = cs_ref[...], sn_ref[...]

    def rope(x):
        return x * cs + pltpu.roll(x, MLA_ROPE, axis=1) * sn

    kr_ref[...] = rope(proj[:, MLA_Q_RANK + MLA_KV_RANK:]).astype(kr_ref.dtype)
    qf = _dot(cq, wuq_ref[...])
    pieces = []
    for h in range(nh):
        lo = h * MLA_QPAD
        pieces.append(qf[:, lo:lo + LANES])
        pieces.append(rope(qf[:, lo + LANES:lo + MLA_QPAD]))
    q_ref[...] = jnp.concatenate(pieces, axis=1).astype(q_ref.dtype)
    kv = _dot(ckv, wukv_ref[...])
    kn_ref[...] = kv[:, :nh * MLA_NOPE].astype(kn_ref.dtype)
    v_ref[...] = kv[:, nh * MLA_NOPE:].astype(v_ref.dtype)


def _mla_proj(h, g, w_in, q_norm_g, w_uq, kv_norm_g, w_ukv, cs, sn):
    t, d = h.shape
    tm, nh = MLA_TILE, MLA_HEADS
    row = lambda n: pl.BlockSpec((tm, n), lambda i: (i, 0))
    return pl.pallas_call(
        _mla_proj_kernel,
        grid=(t // tm,),
        in_specs=[row(d), _const_spec((1, d)), _const_spec(w_in.shape),
                  _const_spec((1, MLA_Q_RANK)), _const_spec(w_uq.shape),
                  _const_spec((1, MLA_KV_RANK)), _const_spec(w_ukv.shape),
                  row(LANES), row(LANES)],
        out_specs=[row(nh * MLA_QPAD), row(nh * MLA_NOPE), row(LANES), row(nh * MLA_V)],
        out_shape=[jax.ShapeDtypeStruct((t, nh * MLA_QPAD), BF16),
                   jax.ShapeDtypeStruct((t, nh * MLA_NOPE), BF16),
                   jax.ShapeDtypeStruct((t, LANES), BF16),
                   jax.ShapeDtypeStruct((t, nh * MLA_V), BF16)],
        compiler_params=_params(1),
        name="mla_proj",
    )(h, g.reshape(1, d), w_in, q_norm_g.reshape(1, -1), w_uq, kv_norm_g.reshape(1, -1), w_ukv, cs, sn)


ATT_TQ = 512
ATT_TK = 512


def _attn_kernel(q_ref, kn_ref, kr_ref, v_ref, wo_ref, h_ref, o_ref, oh_ref):
    tq, tk = ATT_TQ, ATT_TK
    qi = pl.program_id(1)
    ri = lax.broadcasted_iota(jnp.int32, (tq, tk), 0)
    ci = lax.broadcasted_iota(jnp.int32, (tq, tk), 1)
    diag_mask = ci <= ri

    for h in range(MLA_HEADS):
        qh = q_ref[0, :, h * MLA_QPAD:(h + 1) * MLA_QPAD]
        ncol = slice(h * MLA_NOPE, (h + 1) * MLA_NOPE)
        vcol = slice(h * MLA_V, (h + 1) * MLA_V)

        def step(off, carry, mask):
            m, l, acc = carry
            rows = pl.ds(off, tk)
            kj = jnp.concatenate([kn_ref[0, rows, ncol], kr_ref[0, rows, :]], axis=1)
            s = _dot_nt(qh, kj)
            if mask is not None:
                s = jnp.where(mask, s, -jnp.inf)
            m_new = jnp.maximum(m, jnp.max(s, axis=-1, keepdims=True))
            alpha = jnp.exp(m - m_new)
            p = jnp.exp(s - m_new)
            l = alpha * l + jnp.sum(p, axis=-1, keepdims=True)
            acc = alpha * acc + _dot(p.astype(BF16), v_ref[0, rows, vcol])
            return m_new, l, acc

        init = (jnp.full((tq, 1), -jnp.inf, F32), jnp.zeros((tq, 1), F32),
                jnp.zeros((tq, MLA_V), F32))
        carry = lax.fori_loop(
            0, qi, lambda j, c: step(pl.multiple_of(j * tk, tk), c, None), init)
        _, l, acc = step(pl.multiple_of(qi * tk, tk), carry, diag_mask)
        oh_ref[:, vcol] = (acc / l).astype(BF16)

    o_ref[0] = h_ref[0] + _dot(oh_ref[...], wo_ref[...])


def _mla_attn(q, kn, kr, v, w_out, h3):
    bsz, seq, d = h3.shape
    tq = ATT_TQ
    nh = MLA_HEADS
    full = lambda n: pl.BlockSpec((1, seq, n), lambda b, i: (b, 0, 0), pipeline_mode=pl.Buffered(1))
    return pl.pallas_call(
        _attn_kernel,
        grid=(bsz, seq // tq),
        in_specs=[pl.BlockSpec((1, tq, nh * MLA_QPAD), lambda b, i: (b, i, 0)),
                  full(nh * MLA_NOPE), full(LANES), full(nh * MLA_V),
                  _const_spec(w_out.shape),
                  pl.BlockSpec((1, tq, d), lambda b, i: (b, i, 0))],
        out_specs=pl.BlockSpec((1, tq, d), lambda b, i: (b, i, 0)),
        out_shape=jax.ShapeDtypeStruct((bsz, seq, d), F32),
        scratch_shapes=[pltpu.VMEM((tq, nh * MLA_V), BF16)],
        compiler_params=_params(2),
        name="mla_attn",
    )(q, kn, kr, v, w_out, h3)


def _mla_weights(w_in, w_uq, w_ukv):
    nh, r = MLA_HEADS, MLA_ROPE
    qk = MLA_NOPE + MLA_ROPE

    def swap(w):
        return jnp.concatenate([-w[..., r // 2:], w[..., :r // 2]], axis=-1)

    k_rope = w_in[:, MLA_Q_RANK + MLA_KV_RANK:]
    w_in2 = jnp.concatenate([w_in, swap(k_rope)], axis=1)
    wq = w_uq.reshape(MLA_Q_RANK, nh, qk) * (qk ** -0.5)
    wq2 = jnp.concatenate([wq, swap(wq[..., MLA_NOPE:])], axis=-1).reshape(MLA_Q_RANK, nh * MLA_QPAD)
    wkv = w_ukv.reshape(MLA_KV_RANK, nh, 2, MLA_NOPE).transpose(0, 2, 1, 3).reshape(MLA_KV_RANK, -1)
    return w_in2.astype(BF16), wq2.astype(BF16), wkv.astype(BF16)


def _rope_tables(positions):
    r = MLA_ROPE
    freqs = ROPE_THETA ** (-jnp.arange(0, r, 2, dtype=F32) / r)
    ang = positions.astype(F32)[..., None] * freqs
    cos, sin = jnp.cos(ang), jnp.sin(ang)
    pad = jnp.zeros(cos.shape[:-1] + (LANES - r,), F32)
    cs = jnp.concatenate([cos, cos, pad], axis=-1).reshape(-1, LANES)
    sn = jnp.concatenate([sin, sin, pad], axis=-1).reshape(-1, LANES)
    return cs, sn


def kernel(x, positions, norm_mix_g, norm_ffn_g, final_norm_g, even_w_in, hgrn_lb_logits, hgrn_norm_g, s5_a_re, s5_a_im, s5_log_dt, s5_b_re, s5_b_im, s5_c_re, s5_c_im, s5_d, s5_w_glu, s5_b_glu, even_w_out, odd_w_in, mla_q_norm_g, mla_w_uq, mla_kv_norm_g, mla_w_ukv, odd_w_out, ffn_w_in, ffn_conv_w, ffn_conv_b, ffn_w_out):
    bsz, seq, d = x.shape
    t = bsz * seq
    depth = norm_mix_g.shape[0]
    lower_bounds = jnp.cumsum(jax.nn.softmax(hgrn_lb_logits.astype(F32), axis=0), axis=0)
    hgrn_dim = HGRN_HEADS * HGRN_HEAD_DIM
    h = x.reshape(t, d)
    for layer in range(depth):
        j = layer // 2
        if layer % 2 == 0:
            proj = _norm_proj(h, norm_mix_g[layer], even_w_in[j].astype(BF16))
            proj3 = proj.reshape(bsz, seq, -1)
            ya = _hgrn(proj3, lower_bounds[j], hgrn_norm_g[j])
            s5_dim = s5_d.shape[1]
            tables = _s5_tables(s5_a_re[j], s5_a_im[j], s5_log_dt[j], s5_b_re[j], s5_b_im[j],
                                s5_c_re[j], s5_c_im[j])
            yb = _s5(proj3, (4 * hgrn_dim) // s5_dim, tables, s5_d[j], s5_w_glu[j], s5_b_glu[j])
            h = _even_out(ya.reshape(t, -1), yb.reshape(t, -1), even_w_out[j], h)
        else:
            w_in2, wq2, wkv = _mla_weights(odd_w_in[j], mla_w_uq[j], mla_w_ukv[j])
            cs, sn = _rope_tables(positions)
            q, kn, kr, v = _mla_proj(h, norm_mix_g[layer], w_in2, mla_q_norm_g[j], wq2,
                                     mla_kv_norm_g[j], wkv, cs, sn)
            r3 = lambda a: a.reshape(bsz, seq, -1)
            h = _mla_attn(r3(q), r3(kn), r3(kr), r3(v), odd_w_out[j].astype(BF16),
                          h.reshape(bsz, seq, d)).reshape(t, d)
        last = layer == depth - 1
        h = _ffn(h, seq, norm_ffn_g[layer], ffn_w_in[layer], ffn_conv_w[layer], ffn_conv_b[layer],
                 ffn_w_out[layer], final_norm_g if last else None)
    return h.reshape(bsz, seq, d)
```

```python
import functools
import math

import jax
import jax.numpy as jnp
from jax import lax
from jax.experimental import pallas as pl
from jax.experimental.pallas import tpu as pltpu

F32 = jnp.float32
BF16 = jnp.bfloat16

EPS = 1e-6
HGRN_HEADS = 4
HGRN_HEAD_DIM = 128
HGRN_CHUNK = 64
S5_GROUP = 16
S5_STATE = 64
S5_BLOCK_GROUPS = 8
S5_CHUNK = 64
MLA_HEADS = 8
MLA_Q_RANK = 384
MLA_KV_RANK = 256
MLA_NOPE = 128
MLA_ROPE = 64
MLA_V = 128
ROPE_THETA = 10000.0
LANES = 128
MLA_QPAD = 2 * LANES

VMEM_LIMIT = 56 * 1024 * 1024


def _params(n_axes, vmem=VMEM_LIMIT):
    return pltpu.CompilerParams(dimension_semantics=("arbitrary",) * n_axes,
                                vmem_limit_bytes=vmem)


def _const_spec(shape):
    zeros = (0,) * len(shape)
    return pl.BlockSpec(shape, lambda *_: zeros, pipeline_mode=pl.Buffered(1))


def _rms(x, g):
    return x * lax.rsqrt(jnp.mean(x * x, axis=-1, keepdims=True) + EPS) * g


def _sigmoid(x):
    return 1.0 / (1.0 + jnp.exp(-x))


def _dot(a, b):
    return jnp.dot(a, b, preferred_element_type=F32)


def _dot_nt(a, b):
    return lax.dot_general(a, b, (((1,), (1,)), ((), ())), preferred_element_type=F32)


def _dot_tn(a, b):
    return lax.dot_general(a, b, (((0,), (0,)), ((), ())), preferred_element_type=F32)


def _norm_proj_kernel(h_ref, g_ref, w_ref, o_ref):
    hn = _rms(h_ref[...], g_ref[...]).astype(BF16)
    o_ref[...] = _dot(hn, w_ref[...]).astype(o_ref.dtype)


def _norm_proj(h, g, w, tm=512):
    t, d = h.shape
    n = w.shape[1]
    return pl.pallas_call(
        _norm_proj_kernel,
        grid=(t // tm,),
        in_specs=[pl.BlockSpec((tm, d), lambda i: (i, 0)),
                  _const_spec((1, d)),
                  _const_spec((d, n))],
        out_specs=pl.BlockSpec((tm, n), lambda i: (i, 0)),
        out_shape=jax.ShapeDtypeStruct((t, n), F32),
        compiler_params=_params(1),
        name="even_in",
    )(h, g.reshape(1, d), w)


HGRN_TILE = 512
HGRN_ATT_BLOCK = 256


def _hgrn_kernel(q_ref, f_ref, i_ref, g_ref, lb_ref, ng_ref, o_ref, st_ref):
    L, C, AB = HGRN_TILE, HGRN_CHUNK, HGRN_ATT_BLOCK
    dh = HGRN_HEAD_DIM

    nh = HGRN_HEADS

    @pl.when(pl.program_id(1) == 0)
    def _():
        st_ref[...] = jnp.zeros_like(st_ref)

    q = q_ref[0]
    f = f_ref[0]
    g = g_ref[0]
    lb = lb_ref[...]
    forget = lb + (1.0 - lb) * _sigmoid(f)
    kh = 1.0 - forget
    rc = lax.broadcasted_iota(jnp.int32, (L, nh * dh), 0) & (C - 1)
    b = jnp.log(forget)
    s = 1
    while s < C:
        b = b + jnp.where(rc >= s, pltpu.roll(b, s, axis=0), 0.0)
        s *= 2
    qd = (q * jnp.exp(b)).astype(BF16)
    kd = (kh * jnp.exp(-b)).astype(BF16)
    vb = i_ref[0].astype(BF16)

    ri = lax.broadcasted_iota(jnp.int32, (AB, AB), 0)
    ci = lax.broadcasted_iota(jnp.int32, (AB, AB), 1)
    mask = jnp.logical_and((ri // C) == (ci // C), ci <= ri)
    outs = []
    for h in range(nh):
        hc = slice(h * dh, (h + 1) * dh)
        o_intra = []
        for blk in range(L // AB):
            rows = slice(blk * AB, (blk + 1) * AB)
            att = _dot_nt(qd[rows, hc], kd[rows, hc])
            att = jnp.where(mask, att, 0.0).astype(BF16)
            o_intra.append(_dot(att, vb[rows, hc]))
        st = st_ref[h]
        o_inter = []
        for c in range(L // C):
            rows = slice(c * C, (c + 1) * C)
            b_c = b[rows, hc]
            bl = b_c[C - 1:C, :]
            kdec = (kh[rows, hc] * jnp.exp(bl - b_c)).astype(BF16)
            o_inter.append(_dot_nt(qd[rows, hc], st.astype(BF16)))
            st = st * jnp.exp(bl) + _dot_tn(vb[rows, hc], kdec)
        st_ref[h] = st
        o = jnp.concatenate(o_intra, axis=0) + jnp.concatenate(o_inter, axis=0)
        outs.append(o * lax.rsqrt(jnp.mean(o * o, axis=-1, keepdims=True) + EPS))
    o = jnp.concatenate(outs, axis=1) * ng_ref[...]
    o_ref[0] = (o * (g * _sigmoid(g))).astype(o_ref.dtype)


def _hgrn(proj3, lb, norm_g):
    bsz, seq, _ = proj3.shape
    nh, dh, L = HGRN_HEADS, HGRN_HEAD_DIM, HGRN_TILE
    w = nh * dh

    def col(k):
        return pl.BlockSpec((1, L, w), lambda b, s, k=k: (b, s, k))

    return pl.pallas_call(
        _hgrn_kernel,
        grid=(bsz, seq // L),
        in_specs=[col(0), col(1), col(2), col(3), _const_spec((1, w)), _const_spec((1, w))],
        out_specs=pl.BlockSpec((1, L, w), lambda b, s: (b, s, 0)),
        out_shape=jax.ShapeDtypeStruct((bsz, seq, w), BF16),
        scratch_shapes=[pltpu.VMEM((nh, dh, dh), F32)],
        compiler_params=_params(2),
        name="hgrn2",
    )(proj3, proj3, proj3, proj3, lb.reshape(1, w), norm_g.reshape(1, w))


S5_TILE = 256


def _s5_tables(a_re, a_im, log_dt, b_re, b_im, c_re, c_im):
    G, P = a_re.shape
    Hc = b_re.shape[-1]
    GB = S5_BLOCK_GROUPS
    NB = G // GB
    Lc = S5_CHUNK
    dt = jnp.exp(log_dt)[:, None]
    lam, th = a_re * dt, a_im * dt
    mag = jnp.exp(lam)
    abar_re, abar_im = mag * jnp.cos(th), mag * jnp.sin(th)
    den = a_re * a_re + a_im * a_im
    xr, xi = abar_re - 1.0, abar_im
    coef_re = ((xr * a_re + xi * a_im) / den)[..., None]
    coef_im = ((xi * a_re - xr * a_im) / den)[..., None]
    bb_re = coef_re * b_re - coef_im * b_im
    bb_im = coef_re * b_im + coef_im * b_re
    eye = jnp.eye(GB, dtype=F32)

    def blk_in(bb):
        return jnp.einsum('kgph,gj->kghjp', bb.reshape(NB, GB, P, Hc), eye).reshape(NB, GB * Hc, GB * P)

    def blk_out(c):
        return jnp.einsum('kghp,gj->kjpgh', c.reshape(NB, GB, Hc, P), eye).reshape(NB, GB * P, GB * Hc)

    bmat = jnp.concatenate([blk_in(bb_re), blk_in(bb_im)], axis=-1).astype(BF16)
    cmat = jnp.concatenate([blk_out(c_re), -blk_out(c_im)], axis=1).astype(BF16)

    def lay(re, im):
        L = re.shape[0]
        return jnp.concatenate([re.reshape(L, NB, GB * P), im.reshape(L, NB, GB * P)],
                               axis=-1).reshape(L, NB * 2 * GB * P)

    t = jnp.arange(Lc, dtype=F32)[:, None, None]
    pm, im = jnp.exp(t * lam), jnp.exp(-t * lam)
    wp = lay(pm * jnp.cos(t * th), pm * jnp.sin(t * th))
    winv = lay(im * jnp.cos(t * th), -im * jnp.sin(t * th))
    abar1 = lay(abar_re[None], abar_im[None])
    return bmat, cmat, winv, wp, abar1


def _gelu_tanh(x):
    return 0.5 * x * (1.0 + jnp.tanh(math.sqrt(2.0 / math.pi) * (x + 0.044715 * (x * x * x))))


def _s5_kernel(u_ref, bmat_ref, cmat_ref, winv_ref, wp_ref, abar_ref, tri_ref, d_ref,
               wglu_ref, bglu_ref, o_ref, hst_ref):
    TS, Lc = S5_TILE, S5_CHUNK
    nb = bmat_ref.shape[0]
    gw = bmat_ref.shape[1]
    half = bmat_ref.shape[2] // 2

    @pl.when(pl.program_id(1) == 0)
    def _():
        hst_ref[...] = jnp.zeros_like(hst_ref)

    u = u_ref[0]
    ub = u.astype(BF16)
    tri = tri_ref[...]
    ys = []
    for k in range(nb):
        lo = k * 2 * half
        re = slice(lo, lo + half)
        im = slice(lo + half, lo + 2 * half)
        bu = _dot(ub[:, k * gw:(k + 1) * gw], bmat_ref[k])
        wir, wii = winv_ref[:, re], winv_ref[:, im]
        wpr, wpi = wp_ref[:, re], wp_ref[:, im]
        a1r, a1i = abar_ref[:, re], abar_ref[:, im]
        hr_prev, hi_prev = hst_ref[:, re], hst_ref[:, im]
        hs = []
        for c in range(TS // Lc):
            rows = slice(c * Lc, (c + 1) * Lc)
            bur, bui = bu[rows, :half], bu[rows, half:]
            x = jnp.concatenate([bur * wir - bui * wii, bur * wii + bui * wir], axis=1).astype(BF16)
            cs = _dot(tri, x)
            cr = cs[:, :half] + (a1r * hr_prev - a1i * hi_prev)
            cim = cs[:, half:] + (a1r * hi_prev + a1i * hr_prev)
            hr = cr * wpr - cim * wpi
            hi = cr * wpi + cim * wpr
            hr_prev, hi_prev = hr[Lc - 1:Lc, :], hi[Lc - 1:Lc, :]
            hs.append(jnp.concatenate([hr, hi], axis=1).astype(BF16))
        hst_ref[:, re] = hr_prev
        hst_ref[:, im] = hi_prev
        ys.append(_dot(jnp.concatenate(hs, axis=0), cmat_ref[k]))
    y = jnp.concatenate(ys, axis=1) + d_ref[...] * u
    z = _gelu_tanh(y)
    gate = _sigmoid(_dot(z.astype(BF16), wglu_ref[...]) + bglu_ref[...])
    o_ref[0] = (z * gate).astype(o_ref.dtype)


def _s5(proj3, u_col_block, tables, d_skip, w_glu, b_glu):
    bsz, seq, _ = proj3.shape
    bmat, cmat, winv, wp, abar1 = tables
    dim = d_skip.shape[0]
    TS, Lc = S5_TILE, S5_CHUNK
    tri = jnp.tril(jnp.ones((Lc, Lc), F32)).astype(BF16)
    nstate = winv.shape[1]
    return pl.pallas_call(
        _s5_kernel,
        grid=(bsz, seq // TS),
        in_specs=[pl.BlockSpec((1, TS, dim), lambda b, s: (b, s, u_col_block)),
                  _const_spec(bmat.shape), _const_spec(cmat.shape),
                  _const_spec(winv.shape), _const_spec(wp.shape), _const_spec(abar1.shape),
                  _const_spec(tri.shape), _const_spec((1, dim)),
                  _const_spec(w_glu.shape), _const_spec((1, dim))],
        out_specs=pl.BlockSpec((1, TS, dim), lambda b, s: (b, s, 0)),
        out_shape=jax.ShapeDtypeStruct((bsz, seq, dim), BF16),
        scratch_shapes=[pltpu.VMEM((1, nstate), F32)],
        compiler_params=_params(2),
        name="s5",
    )(proj3, bmat, cmat, winv, wp, abar1, tri, d_skip.reshape(1, dim),
      w_glu.astype(BF16), b_glu.reshape(1, dim))


def _even_out_kernel(ya_ref, yb_ref, wa_ref, wb_ref, h_ref, o_ref):
    o_ref[...] = h_ref[...] + _dot(ya_ref[...], wa_ref[...]) + _dot(yb_ref[...], wb_ref[...])


def _even_out(ya, yb, w_out, h, tm=1024):
    t, d = h.shape
    ka, kb = ya.shape[1], yb.shape[1]
    return pl.pallas_call(
        _even_out_kernel,
        grid=(t // tm,),
        in_specs=[pl.BlockSpec((tm, ka), lambda i: (i, 0)),
                  pl.BlockSpec((tm, kb), lambda i: (i, 0)),
                  _const_spec((ka, d)), _const_spec((kb, d)),
                  pl.BlockSpec((tm, d), lambda i: (i, 0))],
        out_specs=pl.BlockSpec((tm, d), lambda i: (i, 0)),
        out_shape=jax.ShapeDtypeStruct((t, d), F32),
        compiler_params=_params(1),
        name="even_out",
    )(ya, yb, w_out[:ka].astype(BF16), w_out[ka:].astype(BF16), h)


FFN_TILE = 512
FFN_CHUNK = 256
SUBLANES = 8


def _ffn_kernel(tiles_per_seq, final_norm, h_ref, g_ref, wa_ref, wu_ref, cw_ref, cb_ref,
                wo_ref, fg_ref, o_ref, prev_ref, abuf_ref, act_ref):
    tm, fc, hal = FFN_TILE, FFN_CHUNK, SUBLANES
    dff = wa_ref.shape[1]

    @pl.when(pl.program_id(0) % tiles_per_seq == 0)
    def _():
        prev_ref[...] = jnp.zeros_like(prev_ref)

    x = h_ref[...]
    hn = _rms(x, g_ref[...]).astype(BF16)
    for c in range(dff // fc):
        cols = slice(c * fc, (c + 1) * fc)
        a = _dot(hn, wa_ref[:, cols])
        u = _dot(hn, wu_ref[:, cols])
        ab = abuf_ref.at[c % 2]
        ab[0:hal, :] = prev_ref[:, cols]
        ab[hal:hal + tm, :] = a
        prev_ref[:, cols] = a[tm - hal:tm, :]
        a1 = ab[hal - 1:hal - 1 + tm, :]
        a2 = ab[hal - 2:hal - 2 + tm, :]
        w = cw_ref[:, cols]
        conv = a * w[2:3, :] + a1 * w[1:2, :] + a2 * w[0:1, :] + cb_ref[:, cols]
        act_ref[:, cols] = (conv * _sigmoid(conv) * u).astype(BF16)
    y = x + _dot(act_ref[...], wo_ref[...])
    if final_norm:
        y = _rms(y, fg_ref[...])
    o_ref[...] = y


def _ffn(h, seq, g, w_in, conv_w, conv_b, w_out, final_g=None):
    t, d = h.shape
    dff = w_out.shape[0]
    tm, fc, hal = FFN_TILE, FFN_CHUNK, SUBLANES
    final_norm = final_g is not None
    fg = final_g if final_norm else g
    w_in = w_in.astype(BF16)
    return pl.pallas_call(
        functools.partial(_ffn_kernel, seq // tm, final_norm),
        grid=(t // tm,),
        in_specs=[pl.BlockSpec((tm, d), lambda i: (i, 0)),
                  _const_spec((1, d)),
                  _const_spec((d, dff)), _const_spec((d, dff)),
                  _const_spec((conv_w.shape[0], dff)), _const_spec((1, dff)),
                  _const_spec((dff, d)), _const_spec((1, d))],
        out_specs=pl.BlockSpec((tm, d), lambda i: (i, 0)),
        out_shape=jax.ShapeDtypeStruct((t, d), F32),
        scratch_shapes=[pltpu.VMEM((hal, dff), F32),
                        pltpu.VMEM((2, hal + tm, fc), F32),
                        pltpu.VMEM((tm, dff), BF16)],
        compiler_params=_params(1),
        name="ffn",
    )(h, g.reshape(1, d), w_in[:, :dff], w_in[:, dff:], conv_w, conv_b.reshape(1, dff),
      w_out.astype(BF16), fg.reshape(1, d))


MLA_TILE = 512


def _mla_proj_kernel(h_ref, g_ref, win_ref, qg_ref, wuq_ref, kvg_ref, wukn_ref, wuvt_ref,
                     cs_ref, sn_ref, q_ref, kn_ref, kr_ref, vt_ref):
    nh = MLA_HEADS
    hn = _rms(h_ref[...], g_ref[...]).astype(BF16)
    proj = _dot(hn, win_ref[...])
    cq = _rms(proj[:, :MLA_Q_RANK], qg_ref[...]).astype(BF16)
    ckv = _rms(proj[:, MLA_Q_RANK:MLA_Q_RANK + MLA_KV_RANK], kvg_ref[...]).astype(BF16)
    cs, sn = cs_ref[...], sn_ref[...]

    def rope(x):
        return x * cs + pltpu.roll(x, MLA_ROPE, axis=1) * sn

    kr_ref[...] = rope(proj[:, MLA_Q_RANK + MLA_KV_RANK:]).astype(kr_ref.dtype)
    qf = _dot(cq, wuq_ref[...])
    pieces = []
    for h in range(nh):
        lo = h * MLA_QPAD
        pieces.append(qf[:, lo:lo + LANES])
        pieces.append(rope(qf[:, lo + LANES:lo + MLA_QPAD]))
    q_ref[...] = jnp.concatenate(pieces, axis=1).astype(q_ref.dtype)
    kn_ref[...] = _dot(ckv, wukn_ref[...]).astype(kn_ref.dtype)
    vt_ref[0] = _dot_nt(wuvt_ref[...], ckv).astype(vt_ref.dtype)


def _mla_proj(h, seq, g, w_in, q_norm_g, w_uq, kv_norm_g, w_ukn, w_uvt, cs, sn):
    t, d = h.shape
    tm, nh = MLA_TILE, MLA_HEADS
    tps = seq // tm
    row = lambda n: pl.BlockSpec((tm, n), lambda i: (i, 0))
    return pl.pallas_call(
        _mla_proj_kernel,
        grid=(t // tm,),
        in_specs=[row(d), _const_spec((1, d)), _const_spec(w_in.shape),
                  _const_spec((1, MLA_Q_RANK)), _const_spec(w_uq.shape),
                  _const_spec((1, MLA_KV_RANK)), _const_spec(w_ukn.shape), _const_spec(w_uvt.shape),
                  row(LANES), row(LANES)],
        out_specs=[row(nh * MLA_QPAD), row(nh * MLA_NOPE), row(LANES),
                   pl.BlockSpec((1, nh * MLA_V, tm), lambda i: (i // tps, 0, i % tps))],
        out_shape=[jax.ShapeDtypeStruct((t, nh * MLA_QPAD), BF16),
                   jax.ShapeDtypeStruct((t, nh * MLA_NOPE), BF16),
                   jax.ShapeDtypeStruct((t, LANES), BF16),
                   jax.ShapeDtypeStruct((t // seq, nh * MLA_V, seq), BF16)],
        compiler_params=_params(1),
        name="mla_proj",
    )(h, g.reshape(1, d), w_in, q_norm_g.reshape(1, -1), w_uq, kv_norm_g.reshape(1, -1),
      w_ukn, w_uvt, cs, sn)


ATT_TQ = 512
ATT_TK = 512
ATT_LOOKAHEAD = 2


def _attn_kernel(q_ref, kn_ref, kr_ref, vt_ref, wo_ref, h_ref, o_ref,
                 ot_ref, m_ref, l_ref, acc_ref, st0_ref):
    tq, tk = ATT_TQ, ATT_TK
    qi = pl.program_id(1)
    ki = lax.broadcasted_iota(jnp.int32, (tk, tq), 0)
    qc = lax.broadcasted_iota(jnp.int32, (tk, tq), 1)
    diag_mask = ki <= qc

    nh = MLA_HEADS
    m_ref[...] = jnp.full(m_ref.shape, -jnp.inf, F32)
    l_ref[...] = jnp.zeros(l_ref.shape, F32)
    acc_ref[...] = jnp.zeros(acc_ref.shape, F32)

    def scores(off, h):
        rows = pl.ds(off, tk)
        kj = jnp.concatenate([kn_ref[0, rows, h * MLA_NOPE:(h + 1) * MLA_NOPE], kr_ref[0, rows, :]],
                             axis=1)
        return _dot_nt(kj, q_ref[0, :, h * MLA_QPAD:(h + 1) * MLA_QPAD])

    def update(off, h, st):
        m = m_ref[h]
        m_new = jnp.maximum(m, jnp.max(st, axis=0, keepdims=True))
        alpha = jnp.exp2(m - m_new)
        p = jnp.exp2(st - m_new)
        m_ref[h] = m_new
        l_ref[h] = alpha * l_ref[h] + jnp.sum(p, axis=0, keepdims=True)
        acc_ref[h] = alpha * acc_ref[h] + _dot(
            vt_ref[0, h * MLA_V:(h + 1) * MLA_V, pl.ds(off, tk)], p.astype(BF16))

    def key_tile(off, off_next, mask):
        pending = {0: st0_ref[...]}
        todo = list(range(1, nh)) + ([None] if off_next is not None else [])

        def issue():
            if todo:
                h = todo.pop(0)
                if h is None:
                    st0_ref[...] = scores(off_next, 0)
                else:
                    pending[h] = scores(off, h)

        for _ in range(ATT_LOOKAHEAD):
            issue()
        for h in range(nh):
            st = pending.pop(h)
            if mask is not None:
                st = jnp.where(mask, st, -jnp.inf)
            update(off, h, st)
            issue()

    st0_ref[...] = scores(0, 0)

    def body(j, carry):
        key_tile(pl.multiple_of(j * tk, tk), pl.multiple_of((j + 1) * tk, tk), None)
        return carry

    lax.fori_loop(0, qi, body, 0)
    key_tile(pl.multiple_of(qi * tk, tk), None, diag_mask)
    for h in range(nh):
        ot_ref[h * MLA_V:(h + 1) * MLA_V, :] = (acc_ref[h] / l_ref[h]).astype(BF16)
    o_ref[0] = h_ref[0] + _dot_tn(ot_ref[...], wo_ref[...])


def _mla_attn(q, kn, kr, vt, w_out, h3):
    bsz, seq, d = h3.shape
    tq = ATT_TQ
    nh = MLA_HEADS
    full = lambda n: pl.BlockSpec((1, seq, n), lambda b, i: (b, 0, 0), pipeline_mode=pl.Buffered(1))
    return pl.pallas_call(
        _attn_kernel,
        grid=(bsz, seq // tq),
        in_specs=[pl.BlockSpec((1, tq, nh * MLA_QPAD), lambda b, i: (b, i, 0)),
                  full(nh * MLA_NOPE), full(LANES),
                  pl.BlockSpec((1, nh * MLA_V, seq), lambda b, i: (b, 0, 0),
                               pipeline_mode=pl.Buffered(1)),
                  _const_spec(w_out.shape),
                  pl.BlockSpec((1, tq, d), lambda b, i: (b, i, 0))],
        out_specs=pl.BlockSpec((1, tq, d), lambda b, i: (b, i, 0)),
        out_shape=jax.ShapeDtypeStruct((bsz, seq, d), F32),
        scratch_shapes=[pltpu.VMEM((nh * MLA_V, tq), BF16),
                        pltpu.VMEM((nh, 1, tq), F32), pltpu.VMEM((nh, 1, tq), F32),
                        pltpu.VMEM((nh, MLA_V, tq), F32),
                        pltpu.VMEM((ATT_TK, tq), F32)],
        compiler_params=_params(2),
        name="mla_attn",
    )(q, kn, kr, vt, w_out, h3)


def _mla_weights(w_in, w_uq, w_ukv):
    nh, r = MLA_HEADS, MLA_ROPE
    qk = MLA_NOPE + MLA_ROPE

    def swap(w):
        return jnp.concatenate([-w[..., r // 2:], w[..., :r // 2]], axis=-1)

    k_rope = w_in[:, MLA_Q_RANK + MLA_KV_RANK:]
    w_in2 = jnp.concatenate([w_in, swap(k_rope)], axis=1)
    wq = w_uq.reshape(MLA_Q_RANK, nh, qk) * (qk ** -0.5 * math.log2(math.e))
    wq2 = jnp.concatenate([wq, swap(wq[..., MLA_NOPE:])], axis=-1).reshape(MLA_Q_RANK, nh * MLA_QPAD)
    wkv = w_ukv.reshape(MLA_KV_RANK, nh, 2, MLA_NOPE)
    w_ukn = wkv[:, :, 0, :].reshape(MLA_KV_RANK, nh * MLA_NOPE)
    w_uvt = wkv[:, :, 1, :].reshape(MLA_KV_RANK, nh * MLA_V).T
    return w_in2.astype(BF16), wq2.astype(BF16), w_ukn.astype(BF16), w_uvt.astype(BF16)


def _rope_tables(positions):
    r = MLA_ROPE
    freqs = ROPE_THETA ** (-jnp.arange(0, r, 2, dtype=F32) / r)
    ang = positions.astype(F32)[..., None] * freqs
    cos, sin = jnp.cos(ang), jnp.sin(ang)
    pad = jnp.zeros(cos.shape[:-1] + (LANES - r,), F32)
    cs = jnp.concatenate([cos, cos, pad], axis=-1).reshape(-1, LANES)
    sn = jnp.concatenate([sin, sin, pad], axis=-1).reshape(-1, LANES)
    return cs, sn


def kernel(x, positions, norm_mix_g, norm_ffn_g, final_norm_g, even_w_in, hgrn_lb_logits, hgrn_norm_g, s5_a_re, s5_a_im, s5_log_dt, s5_b_re, s5_b_im, s5_c_re, s5_c_im, s5_d, s5_w_glu, s5_b_glu, even_w_out, odd_w_in, mla_q_norm_g, mla_w_uq, mla_kv_norm_g, mla_w_ukv, odd_w_out, ffn_w_in, ffn_conv_w, ffn_conv_b, ffn_w_out):
    bsz, seq, d = x.shape
    t = bsz * seq
    depth = norm_mix_g.shape[0]
    lower_bounds = jnp.cumsum(jax.nn.softmax(hgrn_lb_logits.astype(F32), axis=0), axis=0)
    hgrn_dim = HGRN_HEADS * HGRN_HEAD_DIM
    h = x.reshape(t, d)
    for layer in range(depth):
        j = layer // 2
        if layer % 2 == 0:
            proj = _norm_proj(h, norm_mix_g[layer], even_w_in[j].astype(BF16))
            proj3 = proj.reshape(bsz, seq, -1)
            ya = _hgrn(proj3, lower_bounds[j], hgrn_norm_g[j])
            s5_dim = s5_d.shape[1]
            tables = _s5_tables(s5_a_re[j], s5_a_im[j], s5_log_dt[j], s5_b_re[j], s5_b_im[j],
                                s5_c_re[j], s5_c_im[j])
            yb = _s5(proj3, (4 * hgrn_dim) // s5_dim, tables, s5_d[j], s5_w_glu[j], s5_b_glu[j])
            h = _even_out(ya.reshape(t, -1), yb.reshape(t, -1), even_w_out[j], h)
        else:
            w_in2, wq2, w_ukn, w_uvt = _mla_weights(odd_w_in[j], mla_w_uq[j], mla_w_ukv[j])
            cs, sn = _rope_tables(positions)
            q, kn, kr, vt = _mla_proj(h, seq, norm_mix_g[layer], w_in2, mla_q_norm_g[j], wq2,
                                      mla_kv_norm_g[j], w_ukn, w_uvt, cs, sn)
            r3 = lambda a: a.reshape(bsz, seq, -1)
            h = _mla_attn(r3(q), r3(kn), r3(kr), vt, odd_w_out[j].astype(BF16),
                          h.reshape(bsz, seq, d)).reshape(t, d)
        last = layer == depth - 1
        h = _ffn(h, seq, norm_ffn_g[layer], ffn_w_in[layer], ffn_conv_w[layer], ffn_conv_b[layer],
                 ffn_w_out[layer], final_norm_g if last else None)
    return h.reshape(bsz, seq, d)
```

```python
import functools
import math

import jax
import jax.numpy as jnp
from jax import lax
from jax.experimental import pallas as pl
from jax.experimental.pallas import tpu as pltpu

F32 = jnp.float32
BF16 = jnp.bfloat16

EPS = 1e-6
HGRN_HEADS = 4
HGRN_HEAD_DIM = 128
HGRN_CHUNK = 64
S5_GROUP = 16
S5_STATE = 64
MLA_HEADS = 8
MLA_Q_RANK = 384
MLA_KV_RANK = 256
MLA_NOPE = 128
MLA_ROPE = 64
MLA_V = 128
ROPE_THETA = 10000.0
LANES = 128
MLA_QPAD = 2 * LANES
BF16_SUBLANES = 16
MLA_VROWS = MLA_V + BF16_SUBLANES

VMEM_LIMIT = 56 * 1024 * 1024


def _params(n_axes, vmem=VMEM_LIMIT):
    return pltpu.CompilerParams(dimension_semantics=("arbitrary",) * n_axes,
                                vmem_limit_bytes=vmem)


def _const_spec(shape):
    zeros = (0,) * len(shape)
    return pl.BlockSpec(shape, lambda *_: zeros, pipeline_mode=pl.Buffered(1))


def _rms(x, g):
    return x * lax.rsqrt(jnp.mean(x * x, axis=-1, keepdims=True) + EPS) * g


def _sigmoid(x):
    return 0.5 * jnp.tanh(0.5 * x) + 0.5


def _dot(a, b):
    return jnp.dot(a, b, preferred_element_type=F32)


def _dot_nt(a, b):
    return lax.dot_general(a, b, (((1,), (1,)), ((), ())), preferred_element_type=F32)


def _dot_tn(a, b):
    return lax.dot_general(a, b, (((0,), (0,)), ((), ())), preferred_element_type=F32)


def _norm_proj_kernel(h_ref, g_ref, w_ref, o_ref):
    hn = _rms(h_ref[...], g_ref[...]).astype(BF16)
    o_ref[...] = _dot(hn, w_ref[...]).astype(o_ref.dtype)


def _norm_proj(h, g, w, tm=512):
    t, d = h.shape
    n = w.shape[1]
    return pl.pallas_call(
        _norm_proj_kernel,
        grid=(t // tm,),
        in_specs=[pl.BlockSpec((tm, d), lambda i: (i, 0)),
                  _const_spec((1, d)),
                  _const_spec((d, n))],
        out_specs=pl.BlockSpec((tm, n), lambda i: (i, 0)),
        out_shape=jax.ShapeDtypeStruct((t, n), F32),
        compiler_params=_params(1),
        name="even_in",
    )(h, g.reshape(1, d), w)


HGRN_TILE = 512
HGRN_ATT_BLOCK = 256


def _hgrn_kernel(q_ref, f_ref, i_ref, g_ref, lb_ref, ng_ref, o_ref, st_ref):
    L, C, AB = HGRN_TILE, HGRN_CHUNK, HGRN_ATT_BLOCK
    dh = HGRN_HEAD_DIM

    nh = HGRN_HEADS

    @pl.when(pl.program_id(1) == 0)
    def _():
        st_ref[...] = jnp.zeros_like(st_ref)

    q = q_ref[0]
    f = f_ref[0]
    g = g_ref[0]
    lb = lb_ref[...]
    forget = lb + (1.0 - lb) * _sigmoid(f)
    kh = 1.0 - forget
    rc = lax.broadcasted_iota(jnp.int32, (L, nh * dh), 0) & (C - 1)
    b = jnp.log(forget)
    s = 1
    while s < C:
        b = b + jnp.where(rc >= s, pltpu.roll(b, s, axis=0), 0.0)
        s *= 2
    qd = (q * jnp.exp(b)).astype(BF16)
    kd = (kh * jnp.exp(-b)).astype(BF16)
    vb = i_ref[0].astype(BF16)

    ri = lax.broadcasted_iota(jnp.int32, (AB, AB), 0)
    ci = lax.broadcasted_iota(jnp.int32, (AB, AB), 1)
    mask = jnp.logical_and((ri // C) == (ci // C), ci <= ri)
    outs = []
    for h in range(nh):
        hc = slice(h * dh, (h + 1) * dh)
        o_intra = []
        for blk in range(L // AB):
            rows = slice(blk * AB, (blk + 1) * AB)
            att = _dot_nt(qd[rows, hc], kd[rows, hc])
            att = jnp.where(mask, att, 0.0).astype(BF16)
            o_intra.append(_dot(att, vb[rows, hc]))
        st = st_ref[h]
        o_inter = []
        for c in range(L // C):
            rows = slice(c * C, (c + 1) * C)
            b_c = b[rows, hc]
            bl = b_c[C - 1:C, :]
            kdec = (kh[rows, hc] * jnp.exp(bl - b_c)).astype(BF16)
            o_inter.append(_dot_nt(qd[rows, hc], st.astype(BF16)))
            st = st * jnp.exp(bl) + _dot_tn(vb[rows, hc], kdec)
        st_ref[h] = st
        o = jnp.concatenate(o_intra, axis=0) + jnp.concatenate(o_inter, axis=0)
        outs.append(o * lax.rsqrt(jnp.mean(o * o, axis=-1, keepdims=True) + EPS))
    o = jnp.concatenate(outs, axis=1) * ng_ref[...]
    o_ref[0] = (o * (g * _sigmoid(g))).astype(o_ref.dtype)


def _hgrn(proj3, lb, norm_g):
    bsz, seq, _ = proj3.shape
    nh, dh, L = HGRN_HEADS, HGRN_HEAD_DIM, HGRN_TILE
    w = nh * dh

    def col(k):
        return pl.BlockSpec((1, L, w), lambda b, s, k=k: (b, s, k))

    return pl.pallas_call(
        _hgrn_kernel,
        grid=(bsz, seq // L),
        in_specs=[col(0), col(1), col(2), col(3), _const_spec((1, w)), _const_spec((1, w))],
        out_specs=pl.BlockSpec((1, L, w), lambda b, s: (b, s, 0)),
        out_shape=jax.ShapeDtypeStruct((bsz, seq, w), BF16),
        scratch_shapes=[pltpu.VMEM((nh, dh, dh), F32)],
        compiler_params=_params(2),
        name="hgrn2",
    )(proj3, proj3, proj3, proj3, lb.reshape(1, w), norm_g.reshape(1, w))


S5_LC = 16
S5_RC = 16
LANE_BLOCKS = LANES // S5_GROUP


def _s5_tables(a_re, a_im, log_dt, b_re, b_im, c_re, c_im, d_skip):
    G, P = a_re.shape
    Hc = b_re.shape[-1]
    Lc = S5_LC
    hi = lax.Precision.HIGHEST
    dt = jnp.exp(log_dt)[:, None]
    lam, th = a_re * dt, a_im * dt
    mag = jnp.exp(lam)
    abar_re, abar_im = mag * jnp.cos(th), mag * jnp.sin(th)
    den = a_re * a_re + a_im * a_im
    xr, xi = abar_re - 1.0, abar_im
    coef_re = ((xr * a_re + xi * a_im) / den)[..., None]
    coef_im = ((xi * a_re - xr * a_im) / den)[..., None]
    bb_re = coef_re * b_re - coef_im * b_im
    bb_im = coef_re * b_im + coef_im * b_re
    n = jnp.arange(Lc + 1, dtype=F32)[:, None, None]
    pw_re = jnp.exp(n * lam) * jnp.cos(n * th)
    pw_im = jnp.exp(n * lam) * jnp.sin(n * th)
    cp_re = c_re[None] * pw_re[:, :, None, :] - c_im[None] * pw_im[:, :, None, :]
    cp_im = c_re[None] * pw_im[:, :, None, :] + c_im[None] * pw_re[:, :, None, :]
    kern = (jnp.einsum('ngop,gpi->ngoi', cp_re[:Lc], bb_re, precision=hi)
            - jnp.einsum('ngop,gpi->ngoi', cp_im[:Lc], bb_im, precision=hi))
    lag = jnp.arange(Lc)[None, :] - jnp.arange(Lc)[:, None]
    causal = (lag >= 0)[:, :, None, None, None]
    kst = jnp.where(causal, kern[jnp.clip(lag, 0, Lc - 1)], 0.0)
    skip = d_skip.reshape(G, Hc)[:, :, None] * jnp.eye(Hc, dtype=F32)
    kst = kst + jnp.where((lag == 0)[:, :, None, None, None], skip[None, None], 0.0)
    toep = kst.transpose(2, 0, 4, 1, 3).reshape(G, Lc * Hc, Lc * Hc)
    pr = pw_re[Lc - 1 - jnp.arange(Lc)][..., None]
    pi = pw_im[Lc - 1 - jnp.arange(Lc)][..., None]
    bs = jnp.concatenate([pr * bb_re[None] - pi * bb_im[None],
                          pr * bb_im[None] + pi * bb_re[None]], axis=2)
    bs = bs.transpose(1, 0, 3, 2).reshape(G, Lc * Hc, 2 * P)
    cs = jnp.concatenate([cp_re[1:], -cp_im[1:]], axis=3)
    cs = cs.transpose(1, 3, 0, 2).reshape(G, 2 * P, Lc * Hc)
    bs_twin = jnp.concatenate([bs[..., P:], bs[..., :P]], axis=2)
    w1 = jnp.concatenate([toep, bs, bs_twin], axis=2).astype(BF16)
    a_n_re, a_n_im = pw_re[Lc], pw_im[Lc]
    m1 = jnp.concatenate([a_n_re, a_n_re], axis=1).reshape(G, 1, 2 * P)
    m2 = jnp.concatenate([-a_n_im, a_n_im], axis=1).reshape(G, 1, 2 * P)
    return w1, cs.astype(BF16), m1, m2


def _gelu_tanh(x):
    return 0.5 * x * (1.0 + jnp.tanh(math.sqrt(2.0 / math.pi) * (x + 0.044715 * (x * x * x))))


def _transpose_lane_blocks(vs, blk):
    vs = list(vs)
    d = LANE_BLOCKS // 2
    while d >= 1:
        low = (blk & d) == 0
        for i in range(LANE_BLOCKS):
            if i & d == 0:
                a, b = vs[i], vs[i + d]
                vs[i] = jnp.where(low, a, pltpu.roll(b, d * S5_GROUP, axis=1))
                vs[i + d] = jnp.where(low, pltpu.roll(a, LANES - d * S5_GROUP, axis=1), b)
        d //= 2
    return vs


def _s5_kernel(*refs):
    nq = len(refs) - 13
    u_refs = refs[:nq]
    (w1_ref, cs_ref, m1_ref, m2_ref, wglu_ref, bglu_ref, o_ref,
     h_ref, x_ref, y_ref, dh_ref, hs_ref, yt_ref) = refs[nq:]
    nb, tok, _ = u_refs[0].shape
    Lc = S5_LC
    RC = tok // Lc
    R = nb * RC
    G = w1_ref.shape[0]
    gw = Lc * S5_GROUP
    nhalf = Lc // LANE_BLOCKS

    @pl.when(pl.program_id(0) == 0)
    def _():
        h_ref[...] = jnp.zeros_like(h_ref)

    blk = lax.broadcasted_iota(jnp.int32, (R, LANES), 1) // S5_GROUP

    for half in range(nhalf):
        for vq in range(nq):
            ws = _transpose_lane_blocks(
                [u_refs[vq][:, pl.ds(half * LANE_BLOCKS + tl, RC, stride=Lc), :].reshape(R, LANES)
                 for tl in range(LANE_BLOCKS)], blk)
            for gl in range(LANE_BLOCKS):
                col = (nhalf * (LANE_BLOCKS * vq + gl) + half) * LANES
                x_ref[:, col:col + LANES] = ws[gl].astype(BF16)

    sw = 2 * S5_STATE
    for g in range(G):
        r = _dot(x_ref[:, g * gw:(g + 1) * gw], w1_ref[g])
        y_ref[:, g * gw:(g + 1) * gw] = r[:, :gw]
        dh_ref[0, g] = r[:, gw:gw + sw]
        dh_ref[1, g] = r[:, gw + sw:]

    hp = [h_ref[0, g] for g in range(G)]
    hq = [h_ref[1, g] for g in range(G)]
    for c in range(RC):
        rows = pl.ds(c, nb, stride=RC)
        for g in range(G):
            hs_ref[g, rows, :] = hp[g]
            m1, m2 = m1_ref[g], m2_ref[g]
            hp[g], hq[g] = (hp[g] * m1 + hq[g] * m2 + dh_ref[0, g, rows, :],
                            hq[g] * m1 - hp[g] * m2 + dh_ref[1, g, rows, :])
    for g in range(G):
        h_ref[0, g] = hp[g]
        h_ref[1, g] = hq[g]

    for g in range(G):
        y_ref[:, g * gw:(g + 1) * gw] += _dot(hs_ref[g].astype(BF16), cs_ref[g])

    for half in range(nhalf):
        for vq in range(nq):
            cols = [(nhalf * (LANE_BLOCKS * vq + gl) + half) * LANES for gl in range(LANE_BLOCKS)]
            ws = _transpose_lane_blocks([y_ref[:, c0:c0 + LANES] for c0 in cols], blk)
            for tl in range(LANE_BLOCKS):
                yt_ref[vq, :, pl.ds(half * LANE_BLOCKS + tl, RC, stride=Lc), :] = (
                    ws[tl].reshape(nb, RC, LANES))

    for b in range(nb):
        z = _gelu_tanh(jnp.concatenate([yt_ref[vq, b] for vq in range(nq)], axis=1))
        gate = _sigmoid(_dot(z.astype(BF16), wglu_ref[...]) + bglu_ref[...])
        o_ref[b] = (z * gate).astype(o_ref.dtype)


def _s5(proj3, u_col_block, tables, w_glu, b_glu):
    bsz, seq, _ = proj3.shape
    w1, cs, m1, m2 = tables
    dim = w_glu.shape[0]
    tok = S5_RC * S5_LC
    rows = bsz * S5_RC
    G, _, sw = m1.shape
    nq = dim // LANES
    q0 = u_col_block * nq
    u_specs = [pl.BlockSpec((bsz, tok, LANES), lambda i, q=q: (0, i, q0 + q)) for q in range(nq)]
    return pl.pallas_call(
        _s5_kernel,
        grid=(seq // tok,),
        in_specs=u_specs + [_const_spec(w1.shape), _const_spec(cs.shape),
                            _const_spec(m1.shape), _const_spec(m2.shape),
                            _const_spec(w_glu.shape), _const_spec((1, dim))],
        out_specs=pl.BlockSpec((bsz, tok, dim), lambda i: (0, i, 0)),
        out_shape=jax.ShapeDtypeStruct((bsz, seq, dim), BF16),
        scratch_shapes=[pltpu.VMEM((2, G, bsz, sw), F32),
                        pltpu.VMEM((rows, S5_LC * dim), BF16),
                        pltpu.VMEM((rows, S5_LC * dim), F32),
                        pltpu.VMEM((2, G, rows, sw), F32),
                        pltpu.VMEM((G, rows, sw), F32),
                        pltpu.VMEM((nq, bsz, tok, LANES), F32)],
        compiler_params=_params(1),
        name="s5",
    )(*([proj3] * nq), w1, cs, m1, m2, w_glu.astype(BF16), b_glu.reshape(1, dim))


FFN_TILE = 512
FFN_CHUNK = 256
SUBLANES = 8


def _ffn_kernel(tiles_per_seq, final_norm, n_mix, *refs):
    mix = refs[:2 * n_mix]
    (h_ref, g_ref, wa_ref, wu_ref, cw_ref, cb_ref, wo_ref, fg_ref, o_ref,
     prev_ref, abuf_ref, act_ref) = refs[2 * n_mix:]
    tm, fc, hal = FFN_TILE, FFN_CHUNK, SUBLANES
    dff = wa_ref.shape[1]

    @pl.when(pl.program_id(0) % tiles_per_seq == 0)
    def _():
        prev_ref[...] = jnp.zeros_like(prev_ref)

    x = h_ref[...]
    for k in range(n_mix):
        x = x + _dot(mix[2 * k][...], mix[2 * k + 1][...])
    hn = _rms(x, g_ref[...]).astype(BF16)
    for c in range(dff // fc):
        cols = slice(c * fc, (c + 1) * fc)
        a = _dot(hn, wa_ref[:, cols])
        u = _dot(hn, wu_ref[:, cols])
        ab = abuf_ref.at[c % 2]
        ab[0:hal, :] = prev_ref[:, cols]
        ab[hal:hal + tm, :] = a
        prev_ref[:, cols] = a[tm - hal:tm, :]
        a1 = ab[hal - 1:hal - 1 + tm, :]
        a2 = ab[hal - 2:hal - 2 + tm, :]
        w = cw_ref[:, cols]
        conv = a * w[2:3, :] + a1 * w[1:2, :] + a2 * w[0:1, :] + cb_ref[:, cols]
        act_ref[:, cols] = (conv * _sigmoid(conv) * u).astype(BF16)
    y = x + _dot(act_ref[...], wo_ref[...])
    if final_norm:
        y = _rms(y, fg_ref[...])
    o_ref[...] = y


def _ffn(h, seq, g, w_in, conv_w, conv_b, w_out, final_g=None, mix=()):
    t, d = h.shape
    dff = w_out.shape[0]
    tm, fc, hal = FFN_TILE, FFN_CHUNK, SUBLANES
    final_norm = final_g is not None
    fg = final_g if final_norm else g
    w_in = w_in.astype(BF16)
    mix_specs, mix_args = [], []
    for y, w in mix:
        mix_specs += [pl.BlockSpec((tm, y.shape[1]), lambda i: (i, 0)), _const_spec(w.shape)]
        mix_args += [y, w.astype(BF16)]
    return pl.pallas_call(
        functools.partial(_ffn_kernel, seq // tm, final_norm, len(mix)),
        grid=(t // tm,),
        in_specs=mix_specs + [
                  pl.BlockSpec((tm, d), lambda i: (i, 0)),
                  _const_spec((1, d)),
                  pl.BlockSpec((d, dff), lambda i: (0, 0), pipeline_mode=pl.Buffered(1)),
                  pl.BlockSpec((d, dff), lambda i: (0, 1), pipeline_mode=pl.Buffered(1)),
                  _const_spec((conv_w.shape[0], dff)), _const_spec((1, dff)),
                  _const_spec((dff, d)), _const_spec((1, d))],
        out_specs=pl.BlockSpec((tm, d), lambda i: (i, 0)),
        out_shape=jax.ShapeDtypeStruct((t, d), F32),
        scratch_shapes=[pltpu.VMEM((hal, dff), F32),
                        pltpu.VMEM((2, hal + tm, fc), F32),
                        pltpu.VMEM((tm, dff), BF16)],
        compiler_params=_params(1),
        name="ffn",
    )(*mix_args, h, g.reshape(1, d), w_in, w_in, conv_w, conv_b.reshape(1, dff),
      w_out.astype(BF16), fg.reshape(1, d))


MLA_TILE = 512


def _mla_proj_kernel(h_ref, g_ref, win_ref, qg_ref, wuq_ref, kvg_ref, wukn_ref, wuvt_ref,
                     cs_ref, sn_ref, q_ref, kn_ref, kr_ref, vt_ref):
    nh = MLA_HEADS
    hn = _rms(h_ref[...], g_ref[...]).astype(BF16)
    proj = _dot(hn, win_ref[...])
    cq = _rms(proj[:, :MLA_Q_RANK], qg_ref[...]).astype(BF16)
    ckv = _rms(proj[:, MLA_Q_RANK:MLA_Q_RANK + MLA_KV_RANK], kvg_ref[...]).astype(BF16)
    cs, sn = cs_ref[...], sn_ref[...]

    def rope(x):
        return x * cs + pltpu.roll(x, MLA_ROPE, axis=1) * sn

    kr_ref[...] = rope(proj[:, MLA_Q_RANK + MLA_KV_RANK:]).astype(kr_ref.dtype)
    qf = _dot(cq, wuq_ref[...])
    pieces = []
    for h in range(nh):
        lo = h * MLA_QPAD
        pieces.append(qf[:, lo:lo + LANES])
        pieces.append(rope(qf[:, lo + LANES:lo + MLA_QPAD]))
    q_ref[...] = jnp.concatenate(pieces, axis=1).astype(q_ref.dtype)
    kn_ref[...] = _dot(ckv, wukn_ref[...]).astype(kn_ref.dtype)
    vt = _dot_nt(wuvt_ref[...], ckv)
    row = lax.broadcasted_iota(jnp.int32, vt.shape, 0)
    vt_ref[0] = jnp.where(row % MLA_VROWS >= MLA_V, 1.0, vt).astype(vt_ref.dtype)


def _mla_proj(h, seq, g, w_in, q_norm_g, w_uq, kv_norm_g, w_ukn, w_uvt, cs, sn):
    t, d = h.shape
    tm, nh = MLA_TILE, MLA_HEADS
    tps = seq // tm
    row = lambda n: pl.BlockSpec((tm, n), lambda i: (i, 0))
    return pl.pallas_call(
        _mla_proj_kernel,
        grid=(t // tm,),
        in_specs=[row(d), _const_spec((1, d)), _const_spec(w_in.shape),
                  _const_spec((1, MLA_Q_RANK)), _const_spec(w_uq.shape),
                  _const_spec((1, MLA_KV_RANK)), _const_spec(w_ukn.shape), _const_spec(w_uvt.shape),
                  row(LANES), row(LANES)],
        out_specs=[row(nh * MLA_QPAD), row(nh * MLA_NOPE), row(LANES),
                   pl.BlockSpec((1, nh * MLA_VROWS, tm), lambda i: (i // tps, 0, i % tps))],
        out_shape=[jax.ShapeDtypeStruct((t, nh * MLA_QPAD), BF16),
                   jax.ShapeDtypeStruct((t, nh * MLA_NOPE), BF16),
                   jax.ShapeDtypeStruct((t, LANES), BF16),
                   jax.ShapeDtypeStruct((t // seq, nh * MLA_VROWS, seq), BF16)],
        compiler_params=_params(1),
        name="mla_proj",
    )(h, g.reshape(1, d), w_in, q_norm_g.reshape(1, -1), w_uq, kv_norm_g.reshape(1, -1),
      w_ukn, w_uvt, cs, sn)


ATT_TQ = 512
ATT_TK = 512
ATT_LOOKAHEAD = 0


def _attn_kernel(q_ref, kn_ref, kr_ref, vt_ref, wo_ref, h_ref, o_ref,
                 ot_ref, m_ref, acc_ref, st0_ref):
    tq, tk = ATT_TQ, ATT_TK
    qi = pl.program_id(1)
    ki = lax.broadcasted_iota(jnp.int32, (tk, tq), 0)
    qc = lax.broadcasted_iota(jnp.int32, (tk, tq), 1)
    diag_mask = ki <= qc

    nh = MLA_HEADS
    m_ref[...] = jnp.full(m_ref.shape, -jnp.inf, F32)
    acc_ref[...] = jnp.zeros(acc_ref.shape, F32)

    def scores(off, h):
        rows = pl.ds(off, tk)
        kj = jnp.concatenate([kn_ref[0, rows, h * MLA_NOPE:(h + 1) * MLA_NOPE], kr_ref[0, rows, :]],
                             axis=1)
        return _dot_nt(kj, q_ref[0, :, h * MLA_QPAD:(h + 1) * MLA_QPAD])

    def shifted_logits(h, st):
        m = m_ref[h]
        m_new = jnp.maximum(m, jnp.max(st, axis=0, keepdims=True))
        m_ref[h] = m_new
        return jnp.exp2(m - m_new), (st - m_new).astype(BF16)

    def accumulate(off, h, alpha, d):
        acc_ref[h] = alpha * acc_ref[h] + _dot(
            vt_ref[0, h * MLA_VROWS:(h + 1) * MLA_VROWS, pl.ds(off, tk)], jnp.exp2(d))

    def key_tile(off, off_next, mask):
        pending = {0: st0_ref[...]}
        todo = list(range(1, nh)) + ([None] if off_next is not None else [])

        def issue():
            if todo:
                h = todo.pop(0)
                if h is None:
                    st0_ref[...] = scores(off_next, 0)
                else:
                    pending[h] = scores(off, h)

        for _ in range(ATT_LOOKAHEAD):
            issue()
        for h in range(nh):
            st = pending.pop(h)
            if mask is not None:
                st = jnp.where(mask, st, -jnp.inf)
            alpha, d = shifted_logits(h, st)
            issue()
            accumulate(off, h, alpha, d)

    st0_ref[...] = scores(0, 0)

    def body(j, carry):
        key_tile(pl.multiple_of(j * tk, tk), pl.multiple_of((j + 1) * tk, tk), None)
        return carry

    lax.fori_loop(0, qi, body, 0)
    key_tile(pl.multiple_of(qi * tk, tk), None, diag_mask)
    for h in range(nh):
        acc = acc_ref[h]
        ot_ref[h * MLA_V:(h + 1) * MLA_V, :] = (acc[:MLA_V] / acc[MLA_V:MLA_V + 1]).astype(BF16)
    o_ref[0] = h_ref[0] + _dot_tn(ot_ref[...], wo_ref[...])


def _mla_attn(q, kn, kr, vt, w_out, h3):
    bsz, seq, d = h3.shape
    tq = ATT_TQ
    nh = MLA_HEADS
    full = lambda n: pl.BlockSpec((1, seq, n), lambda b, i: (b, 0, 0), pipeline_mode=pl.Buffered(1))
    return pl.pallas_call(
        _attn_kernel,
        grid=(bsz, seq // tq),
        in_specs=[pl.BlockSpec((1, tq, nh * MLA_QPAD), lambda b, i: (b, i, 0)),
                  full(nh * MLA_NOPE), full(LANES),
                  pl.BlockSpec((1, nh * MLA_VROWS, seq), lambda b, i: (b, 0, 0),
                               pipeline_mode=pl.Buffered(1)),
                  _const_spec(w_out.shape),
                  pl.BlockSpec((1, tq, d), lambda b, i: (b, i, 0))],
        out_specs=pl.BlockSpec((1, tq, d), lambda b, i: (b, i, 0)),
        out_shape=jax.ShapeDtypeStruct((bsz, seq, d), F32),
        scratch_shapes=[pltpu.VMEM((nh * MLA_V, tq), BF16),
                        pltpu.VMEM((nh, 1, tq), F32),
                        pltpu.VMEM((nh, MLA_VROWS, tq), F32),
                        pltpu.VMEM((ATT_TK, tq), F32)],
        compiler_params=_params(2),
        name="mla_attn",
    )(q, kn, kr, vt, w_out, h3)


def _mla_weights(w_in, w_uq, w_ukv):
    nh, r = MLA_HEADS, MLA_ROPE
    qk = MLA_NOPE + MLA_ROPE

    def swap(w):
        return jnp.concatenate([-w[..., r // 2:], w[..., :r // 2]], axis=-1)

    k_rope = w_in[:, MLA_Q_RANK + MLA_KV_RANK:]
    w_in2 = jnp.concatenate([w_in, swap(k_rope)], axis=1)
    wq = w_uq.reshape(MLA_Q_RANK, nh, qk) * (qk ** -0.5 * math.log2(math.e))
    wq2 = jnp.concatenate([wq, swap(wq[..., MLA_NOPE:])], axis=-1).reshape(MLA_Q_RANK, nh * MLA_QPAD)
    wkv = w_ukv.reshape(MLA_KV_RANK, nh, 2, MLA_NOPE)
    w_ukn = wkv[:, :, 0, :].reshape(MLA_KV_RANK, nh * MLA_NOPE)
    w_uvt = jnp.pad(wkv[:, :, 1, :], ((0, 0), (0, 0), (0, MLA_VROWS - MLA_V)))
    w_uvt = w_uvt.reshape(MLA_KV_RANK, nh * MLA_VROWS).T
    return w_in2.astype(BF16), wq2.astype(BF16), w_ukn.astype(BF16), w_uvt.astype(BF16)


def _rope_tables(positions):
    r = MLA_ROPE
    freqs = ROPE_THETA ** (-jnp.arange(0, r, 2, dtype=F32) / r)
    ang = positions.astype(F32)[..., None] * freqs
    cos, sin = jnp.cos(ang), jnp.sin(ang)
    pad = jnp.zeros(cos.shape[:-1] + (LANES - r,), F32)
    cs = jnp.concatenate([cos, cos, pad], axis=-1).reshape(-1, LANES)
    sn = jnp.concatenate([sin, sin, pad], axis=-1).reshape(-1, LANES)
    return cs, sn


def kernel(x, positions, norm_mix_g, norm_ffn_g, final_norm_g, even_w_in, hgrn_lb_logits, hgrn_norm_g, s5_a_re, s5_a_im, s5_log_dt, s5_b_re, s5_b_im, s5_c_re, s5_c_im, s5_d, s5_w_glu, s5_b_glu, even_w_out, odd_w_in, mla_q_norm_g, mla_w_uq, mla_kv_norm_g, mla_w_ukv, odd_w_out, ffn_w_in, ffn_conv_w, ffn_conv_b, ffn_w_out):
    bsz, seq, d = x.shape
    t = bsz * seq
    depth = norm_mix_g.shape[0]
    lower_bounds = jnp.cumsum(jax.nn.softmax(hgrn_lb_logits.astype(F32), axis=0), axis=0)
    hgrn_dim = HGRN_HEADS * HGRN_HEAD_DIM
    h = x.reshape(t, d)
    for layer in range(depth):
        j = layer // 2
        mix = ()
        if layer % 2 == 0:
            proj = _norm_proj(h, norm_mix_g[layer], even_w_in[j].astype(BF16))
            proj3 = proj.reshape(bsz, seq, -1)
            ya = _hgrn(proj3, lower_bounds[j], hgrn_norm_g[j])
            s5_dim = s5_d.shape[1]
            tables = _s5_tables(s5_a_re[j], s5_a_im[j], s5_log_dt[j], s5_b_re[j], s5_b_im[j],
                                s5_c_re[j], s5_c_im[j], s5_d[j])
            yb = _s5(proj3, (4 * hgrn_dim) // s5_dim, tables, s5_w_glu[j], s5_b_glu[j])
            mix = ((ya.reshape(t, -1), even_w_out[j][:hgrn_dim]),
                   (yb.reshape(t, -1), even_w_out[j][hgrn_dim:]))
        else:
            w_in2, wq2, w_ukn, w_uvt = _mla_weights(odd_w_in[j], mla_w_uq[j], mla_w_ukv[j])
            cs, sn = _rope_tables(positions)
            q, kn, kr, vt = _mla_proj(h, seq, norm_mix_g[layer], w_in2, mla_q_norm_g[j], wq2,
                                      mla_kv_norm_g[j], w_ukn, w_uvt, cs, sn)
            r3 = lambda a: a.reshape(bsz, seq, -1)
            h = _mla_attn(r3(q), r3(kn), r3(kr), vt, odd_w_out[j].astype(BF16),
                          h.reshape(bsz, seq, d)).reshape(t, d)
        last = layer == depth - 1
        h = _ffn(h, seq, norm_ffn_g[layer], ffn_w_in[layer], ffn_conv_w[layer], ffn_conv_b[layer],
                 ffn_w_out[layer], final_norm_g if last else None, mix)
    return h.reshape(bsz, seq, d)
```

```python
import functools
import math

import jax
import jax.numpy as jnp
from jax import lax
from jax.experimental import pallas as pl
from jax.experimental.pallas import tpu as pltpu

F32 = jnp.float32
BF16 = jnp.bfloat16

EPS = 1e-6
HGRN_HEADS = 4
HGRN_HEAD_DIM = 128
HGRN_CHUNK = 64
S5_GROUP = 16
S5_STATE = 64
MLA_HEADS = 8
MLA_Q_RANK = 384
MLA_KV_RANK = 256
MLA_NOPE = 128
MLA_ROPE = 64
MLA_V = 128
ROPE_THETA = 10000.0
LANES = 128
MLA_QPAD = 2 * LANES
BF16_SUBLANES = 16
MLA_VROWS = MLA_V + BF16_SUBLANES

VMEM_LIMIT = 56 * 1024 * 1024


def _params(n_axes, vmem=VMEM_LIMIT):
    return pltpu.CompilerParams(dimension_semantics=("arbitrary",) * n_axes,
                                vmem_limit_bytes=vmem)


def _const_spec(shape):
    zeros = (0,) * len(shape)
    return pl.BlockSpec(shape, lambda *_: zeros, pipeline_mode=pl.Buffered(1))


def _rms(x, g):
    return x * lax.rsqrt(jnp.mean(x * x, axis=-1, keepdims=True) + EPS) * g


def _sigmoid(x):
    return 0.5 * jnp.tanh(0.5 * x) + 0.5


def _dot(a, b):
    return jnp.dot(a, b, preferred_element_type=F32)


def _dot_nt(a, b):
    return lax.dot_general(a, b, (((1,), (1,)), ((), ())), preferred_element_type=F32)


def _dot_tn(a, b):
    return lax.dot_general(a, b, (((0,), (0,)), ((), ())), preferred_element_type=F32)


def _norm_proj_kernel(h_ref, g_ref, w_ref, o_ref):
    hn = _rms(h_ref[...], g_ref[...]).astype(BF16)
    o_ref[...] = _dot(hn, w_ref[...]).astype(o_ref.dtype)


def _norm_proj(h, g, w, tm=512):
    t, d = h.shape
    n = w.shape[1]
    return pl.pallas_call(
        _norm_proj_kernel,
        grid=(t // tm,),
        in_specs=[pl.BlockSpec((tm, d), lambda i: (i, 0)),
                  _const_spec((1, d)),
                  _const_spec((d, n))],
        out_specs=pl.BlockSpec((tm, n), lambda i: (i, 0)),
        out_shape=jax.ShapeDtypeStruct((t, n), F32),
        compiler_params=_params(1),
        name="even_in",
    )(h, g.reshape(1, d), w)


HGRN_TILE = 512
HGRN_ATT_BLOCK = 256


def _hgrn_kernel(q_ref, f_ref, i_ref, g_ref, lb_ref, ng_ref, o_ref, st_ref):
    L, C, AB = HGRN_TILE, HGRN_CHUNK, HGRN_ATT_BLOCK
    dh = HGRN_HEAD_DIM

    nh = HGRN_HEADS

    @pl.when(pl.program_id(1) == 0)
    def _():
        st_ref[...] = jnp.zeros_like(st_ref)

    q = q_ref[0]
    f = f_ref[0]
    g = g_ref[0]
    lb = lb_ref[...]
    forget = lb + (1.0 - lb) * _sigmoid(f)
    kh = 1.0 - forget
    ri = lax.broadcasted_iota(jnp.int32, (AB, AB), 0)
    ci = lax.broadcasted_iota(jnp.int32, (AB, AB), 1)
    mask = jnp.logical_and((ri // C) == (ci // C), ci <= ri)

    logf = jnp.log(forget)
    lf_hi = logf.astype(BF16)
    lf_lo = (logf - lf_hi.astype(F32)).astype(BF16)
    tri = jnp.where(mask, 1.0, 0.0).astype(BF16)
    b = jnp.concatenate(
        [_dot(tri, lf_hi[r0:r0 + AB]) + _dot(tri, lf_lo[r0:r0 + AB]) for r0 in range(0, L, AB)],
        axis=0)
    qd = (q * jnp.exp(b)).astype(BF16)
    kd = (kh * jnp.exp(-b)).astype(BF16)
    vb = i_ref[0].astype(BF16)
    nc = L // C
    chunk = [slice(c * C, (c + 1) * C) for c in range(nc)]
    head = [slice(h * dh, (h + 1) * dh) for h in range(nh)]
    bl = [b[c * C + C - 1:(c + 1) * C, :] for c in range(nc)]
    kdec = (kh * jnp.exp(jnp.concatenate(
        [jnp.broadcast_to(bl[c], (C, nh * dh)) for c in range(nc)], axis=0) - b)).astype(BF16)
    decay = [jnp.exp(x) for x in bl]
    ds = [[_dot_tn(vb[chunk[c], head[h]], kdec[chunk[c], head[h]]) for h in range(nh)]
          for c in range(nc)]
    st = [st_ref[h] for h in range(nh)]
    st_in = []
    for c in range(nc):
        st_in.append([s.astype(BF16) for s in st])
        st = [st[h] * decay[c][:, head[h]] + ds[c][h] for h in range(nh)]
    for h in range(nh):
        st_ref[h] = st[h]
    outs = []
    for h in range(nh):
        o_inter = [_dot_nt(qd[chunk[c], head[h]], st_in[c][h]) for c in range(nc)]
        o_intra = []
        for blk in range(L // AB):
            rows = slice(blk * AB, (blk + 1) * AB)
            att = _dot_nt(qd[rows, head[h]], kd[rows, head[h]])
            att = jnp.where(mask, att, 0.0).astype(BF16)
            o_intra.append(_dot(att, vb[rows, head[h]]))
        o = jnp.concatenate(o_intra, axis=0) + jnp.concatenate(o_inter, axis=0)
        outs.append(o * lax.rsqrt(jnp.mean(o * o, axis=-1, keepdims=True) + EPS))
    o = jnp.concatenate(outs, axis=1) * ng_ref[...]
    o_ref[0] = (o * (g * _sigmoid(g))).astype(o_ref.dtype)


def _hgrn(proj3, lb, norm_g):
    bsz, seq, _ = proj3.shape
    nh, dh, L = HGRN_HEADS, HGRN_HEAD_DIM, HGRN_TILE
    w = nh * dh

    def col(k):
        return pl.BlockSpec((1, L, w), lambda b, s, k=k: (b, s, k))

    return pl.pallas_call(
        _hgrn_kernel,
        grid=(bsz, seq // L),
        in_specs=[col(0), col(1), col(2), col(3), _const_spec((1, w)), _const_spec((1, w))],
        out_specs=pl.BlockSpec((1, L, w), lambda b, s: (b, s, 0)),
        out_shape=jax.ShapeDtypeStruct((bsz, seq, w), BF16),
        scratch_shapes=[pltpu.VMEM((nh, dh, dh), F32)],
        compiler_params=_params(2),
        name="hgrn2",
    )(proj3, proj3, proj3, proj3, lb.reshape(1, w), norm_g.reshape(1, w))


S5_LC = 16
S5_RC = 16
LANE_BLOCKS = LANES // S5_GROUP


def _s5_tables(a_re, a_im, log_dt, b_re, b_im, c_re, c_im, d_skip):
    G, P = a_re.shape
    Hc = b_re.shape[-1]
    Lc = S5_LC
    dt = jnp.exp(log_dt)[:, None]
    lam, th = a_re * dt, a_im * dt
    mag = jnp.exp(lam)
    abar_re, abar_im = mag * jnp.cos(th), mag * jnp.sin(th)
    den = a_re * a_re + a_im * a_im
    xr, xi = abar_re - 1.0, abar_im
    coef_re = ((xr * a_re + xi * a_im) / den)[..., None]
    coef_im = ((xi * a_re - xr * a_im) / den)[..., None]
    bb_re = coef_re * b_re - coef_im * b_im
    bb_im = coef_re * b_im + coef_im * b_re
    n = jnp.arange(Lc + 1, dtype=F32)[:, None, None]
    pw_re = jnp.exp(n * lam) * jnp.cos(n * th)
    pw_im = jnp.exp(n * lam) * jnp.sin(n * th)
    bt_re, bt_im = bb_re.transpose(0, 2, 1), bb_im.transpose(0, 2, 1)
    cp_re = c_re[None] * pw_re[:Lc, :, None, :] - c_im[None] * pw_im[:Lc, :, None, :]
    cp_im = c_re[None] * pw_im[:Lc, :, None, :] + c_im[None] * pw_re[:Lc, :, None, :]
    kern = jnp.sum(cp_re[:, :, :, None, :] * bt_re[None, :, None, :, :]
                   - cp_im[:, :, :, None, :] * bt_im[None, :, None, :, :], axis=-1)
    skip = d_skip.reshape(G, Hc)[:, :, None] * jnp.eye(Hc, dtype=F32)
    kern = kern.at[0].add(skip)
    kcat = kern.transpose(1, 3, 0, 2).reshape(G, Hc, Lc * Hc)
    toep = jnp.stack([jnp.pad(kcat, ((0, 0), (0, 0), (s * Hc, 0)))[:, :, :Lc * Hc]
                      for s in range(Lc)], axis=1)
    toep = toep.reshape(G, Lc * Hc, Lc * Hc)
    pr = pw_re[:Lc][::-1].transpose(1, 0, 2)[:, :, None, :]
    pi = pw_im[:Lc][::-1].transpose(1, 0, 2)[:, :, None, :]
    bs_re = pr * bt_re[:, None] - pi * bt_im[:, None]
    bs_im = pr * bt_im[:, None] + pi * bt_re[:, None]
    bs = jnp.concatenate([bs_re, bs_im], axis=-1).reshape(G, Lc * Hc, 2 * P)
    bs_twin = jnp.concatenate([bs_im, bs_re], axis=-1).reshape(G, Lc * Hc, 2 * P)
    ct_re = c_re.transpose(0, 2, 1)[:, :, None, :]
    ct_im = c_im.transpose(0, 2, 1)[:, :, None, :]
    qr = pw_re[1:].transpose(1, 2, 0)[..., None]
    qi = pw_im[1:].transpose(1, 2, 0)[..., None]
    cs = jnp.concatenate([ct_re * qr - ct_im * qi, -(ct_re * qi + ct_im * qr)],
                         axis=1).reshape(G, 2 * P, Lc * Hc)
    w1 = jnp.concatenate([toep, bs, bs_twin], axis=2).astype(BF16)
    a_n_re, a_n_im = pw_re[Lc], pw_im[Lc]
    m1 = jnp.concatenate([a_n_re, a_n_re], axis=1).reshape(G, 1, 2 * P)
    m2 = jnp.concatenate([-a_n_im, a_n_im], axis=1).reshape(G, 1, 2 * P)
    return w1, cs.astype(BF16), m1, m2


def _gelu_tanh(x):
    return 0.5 * x * (1.0 + jnp.tanh(math.sqrt(2.0 / math.pi) * (x + 0.044715 * (x * x * x))))


def _transpose_lane_blocks(vs, blk):
    vs = list(vs)
    d = LANE_BLOCKS // 2
    while d >= 1:
        low = (blk & d) == 0
        for i in range(LANE_BLOCKS):
            if i & d == 0:
                a, b = vs[i], vs[i + d]
                vs[i] = jnp.where(low, a, pltpu.roll(b, d * S5_GROUP, axis=1))
                vs[i + d] = jnp.where(low, pltpu.roll(a, LANES - d * S5_GROUP, axis=1), b)
        d //= 2
    return vs


def _s5_kernel(*refs):
    nq = len(refs) - 13
    u_refs = refs[:nq]
    (w1_ref, cs_ref, m1_ref, m2_ref, wglu_ref, bglu_ref, o_ref,
     h_ref, x_ref, y_ref, dh_ref, hs_ref, yt_ref) = refs[nq:]
    nb, tok, _ = u_refs[0].shape
    Lc = S5_LC
    RC = tok // Lc
    R = nb * RC
    G = w1_ref.shape[0]
    gw = Lc * S5_GROUP
    nhalf = Lc // LANE_BLOCKS

    @pl.when(pl.program_id(0) == 0)
    def _():
        h_ref[...] = jnp.zeros_like(h_ref)

    blk = lax.broadcasted_iota(jnp.int32, (R, LANES), 1) // S5_GROUP

    for half in range(nhalf):
        for vq in range(nq):
            ws = _transpose_lane_blocks(
                [u_refs[vq][:, pl.ds(half * LANE_BLOCKS + tl, RC, stride=Lc), :].reshape(R, LANES)
                 for tl in range(LANE_BLOCKS)], blk)
            for gl in range(LANE_BLOCKS):
                col = (nhalf * (LANE_BLOCKS * vq + gl) + half) * LANES
                x_ref[:, col:col + LANES] = ws[gl].astype(BF16)

    sw = 2 * S5_STATE
    for g in range(G):
        r = _dot(x_ref[:, g * gw:(g + 1) * gw], w1_ref[g])
        y_ref[:, g * gw:(g + 1) * gw] = r[:, :gw]
        dh_ref[0, g] = r[:, gw:gw + sw]
        dh_ref[1, g] = r[:, gw + sw:]

    hp = [h_ref[0, g] for g in range(G)]
    hq = [h_ref[1, g] for g in range(G)]
    for c in range(RC):
        rows = pl.ds(c, nb, stride=RC)
        for g in range(G):
            hs_ref[g, rows, :] = hp[g]
            m1, m2 = m1_ref[g], m2_ref[g]
            hp[g], hq[g] = (hp[g] * m1 + hq[g] * m2 + dh_ref[0, g, rows, :],
                            hq[g] * m1 - hp[g] * m2 + dh_ref[1, g, rows, :])
    for g in range(G):
        h_ref[0, g] = hp[g]
        h_ref[1, g] = hq[g]

    for g in range(G):
        y_ref[:, g * gw:(g + 1) * gw] += _dot(hs_ref[g].astype(BF16), cs_ref[g])

    for half in range(nhalf):
        for vq in range(nq):
            cols = [(nhalf * (LANE_BLOCKS * vq + gl) + half) * LANES for gl in range(LANE_BLOCKS)]
            ws = _transpose_lane_blocks([y_ref[:, c0:c0 + LANES] for c0 in cols], blk)
            for tl in range(LANE_BLOCKS):
                yt_ref[vq, :, pl.ds(half * LANE_BLOCKS + tl, RC, stride=Lc), :] = (
                    ws[tl].reshape(nb, RC, LANES))

    for b in range(nb):
        z = _gelu_tanh(jnp.concatenate([yt_ref[vq, b] for vq in range(nq)], axis=1))
        gate = _sigmoid(_dot(z.astype(BF16), wglu_ref[...]) + bglu_ref[...])
        o_ref[b] = (z * gate).astype(o_ref.dtype)


def _s5(proj3, u_col_block, tables, w_glu, b_glu):
    bsz, seq, _ = proj3.shape
    w1, cs, m1, m2 = tables
    dim = w_glu.shape[0]
    tok = S5_RC * S5_LC
    rows = bsz * S5_RC
    G, _, sw = m1.shape
    nq = dim // LANES
    q0 = u_col_block * nq
    u_specs = [pl.BlockSpec((bsz, tok, LANES), lambda i, q=q: (0, i, q0 + q)) for q in range(nq)]
    return pl.pallas_call(
        _s5_kernel,
        grid=(seq // tok,),
        in_specs=u_specs + [_const_spec(w1.shape), _const_spec(cs.shape),
                            _const_spec(m1.shape), _const_spec(m2.shape),
                            _const_spec(w_glu.shape), _const_spec((1, dim))],
        out_specs=pl.BlockSpec((bsz, tok, dim), lambda i: (0, i, 0)),
        out_shape=jax.ShapeDtypeStruct((bsz, seq, dim), BF16),
        scratch_shapes=[pltpu.VMEM((2, G, bsz, sw), F32),
                        pltpu.VMEM((rows, S5_LC * dim), BF16),
                        pltpu.VMEM((rows, S5_LC * dim), F32),
                        pltpu.VMEM((2, G, rows, sw), F32),
                        pltpu.VMEM((G, rows, sw), F32),
                        pltpu.VMEM((nq, bsz, tok, LANES), F32)],
        compiler_params=_params(1),
        name="s5",
    )(*([proj3] * nq), w1, cs, m1, m2, w_glu.astype(BF16), b_glu.reshape(1, dim))


FFN_TILE = 512
FFN_CHUNK = 256
SUBLANES = 8


def _ffn_kernel(tiles_per_seq, final_norm, n_mix, *refs):
    mix = refs[:2 * n_mix]
    (h_ref, g_ref, wa_ref, wu_ref, cw_ref, cb_ref, wo_ref, fg_ref, o_ref,
     prev_ref, abuf_ref, act_ref) = refs[2 * n_mix:]
    tm, fc, hal = FFN_TILE, FFN_CHUNK, SUBLANES
    dff = wa_ref.shape[1]

    @pl.when(pl.program_id(0) % tiles_per_seq == 0)
    def _():
        prev_ref[...] = jnp.zeros_like(prev_ref)

    x = h_ref[...]
    for k in range(n_mix):
        x = x + _dot(mix[2 * k][...], mix[2 * k + 1][...])
    hn = _rms(x, g_ref[...]).astype(BF16)
    for c in range(dff // fc):
        cols = slice(c * fc, (c + 1) * fc)
        a = _dot(hn, wa_ref[:, cols])
        u = _dot(hn, wu_ref[:, cols])
        ab = abuf_ref.at[c % 2]
        ab[0:hal, :] = prev_ref[:, cols]
        ab[hal:hal + tm, :] = a
        prev_ref[:, cols] = a[tm - hal:tm, :]
        a1 = ab[hal - 1:hal - 1 + tm, :]
        a2 = ab[hal - 2:hal - 2 + tm, :]
        w = cw_ref[:, cols]
        conv = a * w[2:3, :] + a1 * w[1:2, :] + a2 * w[0:1, :] + cb_ref[:, cols]
        act_ref[:, cols] = (conv * _sigmoid(conv) * u).astype(BF16)
    y = x + _dot(act_ref[...], wo_ref[...])
    if final_norm:
        y = _rms(y, fg_ref[...])
    o_ref[...] = y


def _ffn(h, seq, layer, g, w_in, conv_w, conv_b, w_out, final_g=None, mix=()):
    t, d = h.shape
    dff = w_out.shape[1]
    tm, fc, hal = FFN_TILE, FFN_CHUNK, SUBLANES
    final_norm = final_g is not None
    fg = final_g if final_norm else g
    mix_specs, mix_args = [], []
    for y, w in mix:
        mix_specs += [pl.BlockSpec((tm, y.shape[1]), lambda i: (i, 0)), _const_spec(w.shape)]
        mix_args += [y, w.astype(BF16)]
    return pl.pallas_call(
        functools.partial(_ffn_kernel, seq // tm, final_norm, len(mix)),
        grid=(t // tm,),
        in_specs=mix_specs + [
                  pl.BlockSpec((tm, d), lambda i: (i, 0)),
                  _const_spec((1, d)),
                  pl.BlockSpec((None, d, dff), lambda i: (layer, 0, 0), pipeline_mode=pl.Buffered(1)),
                  pl.BlockSpec((None, d, dff), lambda i: (layer, 0, 1), pipeline_mode=pl.Buffered(1)),
                  _const_spec((conv_w.shape[0], dff)), _const_spec((1, dff)),
                  pl.BlockSpec((None, dff, d), lambda i: (layer, 0, 0), pipeline_mode=pl.Buffered(1)),
                  _const_spec((1, d))],
        out_specs=pl.BlockSpec((tm, d), lambda i: (i, 0)),
        out_shape=jax.ShapeDtypeStruct((t, d), F32),
        scratch_shapes=[pltpu.VMEM((hal, dff), F32),
                        pltpu.VMEM((2, hal + tm, fc), F32),
                        pltpu.VMEM((tm, dff), BF16)],
        compiler_params=_params(1),
        name="ffn",
    )(*mix_args, h, g.reshape(1, d), w_in, w_in, conv_w, conv_b.reshape(1, dff),
      w_out, fg.reshape(1, d))


MLA_TILE = 512


def _mla_proj_kernel(h_ref, g_ref, win_ref, qg_ref, wuq_ref, kvg_ref, wukn_ref, wuvt_ref,
                     rope_ref, q_ref, kn_ref, kr_ref, vt_ref):
    nh = MLA_HEADS
    hn = _rms(h_ref[...], g_ref[...]).astype(BF16)
    proj = _dot(hn, win_ref[...])
    cq = _rms(proj[:, :MLA_Q_RANK], qg_ref[...]).astype(BF16)
    ckv = _rms(proj[:, MLA_Q_RANK:MLA_Q_RANK + MLA_KV_RANK], kvg_ref[...]).astype(BF16)
    tab = rope_ref[...]

    def rope(x):
        y = x * tab
        return y + pltpu.roll(y, MLA_ROPE, axis=1)

    lane = lax.broadcasted_iota(jnp.int32, tab.shape, 1)
    kr = jnp.where(lane < MLA_ROPE, rope(proj[:, MLA_Q_RANK + MLA_KV_RANK:]), 0.0)
    kr_ref[...] = kr.astype(kr_ref.dtype)
    qf = _dot(cq, wuq_ref[...])
    pieces = []
    for h in range(nh):
        lo = h * MLA_QPAD
        pieces.append(qf[:, lo:lo + LANES])
        pieces.append(rope(qf[:, lo + LANES:lo + MLA_QPAD]))
    q_ref[...] = jnp.concatenate(pieces, axis=1).astype(q_ref.dtype)
    kn_ref[...] = _dot(ckv, wukn_ref[...]).astype(kn_ref.dtype)
    vt = _dot_nt(wuvt_ref[...], ckv)
    row = lax.broadcasted_iota(jnp.int32, vt.shape, 0)
    vt_ref[0] = jnp.where(row % MLA_VROWS >= MLA_V, 1.0, vt).astype(vt_ref.dtype)


def _mla_proj(h, seq, g, w_in, q_norm_g, w_uq, kv_norm_g, w_ukn, w_uvt, rope_tab):
    t, d = h.shape
    tm, nh = MLA_TILE, MLA_HEADS
    tps = seq // tm
    row = lambda n: pl.BlockSpec((tm, n), lambda i: (i, 0))
    return pl.pallas_call(
        _mla_proj_kernel,
        grid=(t // tm,),
        in_specs=[row(d), _const_spec((1, d)), _const_spec(w_in.shape),
                  _const_spec((1, MLA_Q_RANK)), _const_spec(w_uq.shape),
                  _const_spec((1, MLA_KV_RANK)), _const_spec(w_ukn.shape), _const_spec(w_uvt.shape),
                  row(LANES)],
        out_specs=[row(nh * MLA_QPAD), row(nh * MLA_NOPE), row(LANES),
                   pl.BlockSpec((1, nh * MLA_VROWS, tm), lambda i: (i // tps, 0, i % tps))],
        out_shape=[jax.ShapeDtypeStruct((t, nh * MLA_QPAD), BF16),
                   jax.ShapeDtypeStruct((t, nh * MLA_NOPE), BF16),
                   jax.ShapeDtypeStruct((t, LANES), BF16),
                   jax.ShapeDtypeStruct((t // seq, nh * MLA_VROWS, seq), BF16)],
        compiler_params=_params(1),
        name="mla_proj",
    )(h, g.reshape(1, d), w_in, q_norm_g.reshape(1, -1), w_uq, kv_norm_g.reshape(1, -1),
      w_ukn, w_uvt, rope_tab)


ATT_TQ = 512
ATT_TK = 512
ATT_LOOKAHEAD = 0


def _attn_kernel(q_ref, kn_ref, kr_ref, vt_ref, wo_ref, h_ref, o_ref,
                 ot_ref, m_ref, acc_ref, st0_ref):
    tq, tk = ATT_TQ, ATT_TK
    qi = pl.program_id(1)
    ki = lax.broadcasted_iota(jnp.int32, (tk, tq), 0)
    qc = lax.broadcasted_iota(jnp.int32, (tk, tq), 1)
    diag_mask = ki <= qc

    nh = MLA_HEADS
    m_ref[...] = jnp.full(m_ref.shape, -jnp.inf, F32)
    acc_ref[...] = jnp.zeros(acc_ref.shape, F32)

    def scores(off, h):
        rows = pl.ds(off, tk)
        kj = jnp.concatenate([kn_ref[0, rows, h * MLA_NOPE:(h + 1) * MLA_NOPE], kr_ref[0, rows, :]],
                             axis=1)
        return _dot_nt(kj, q_ref[0, :, h * MLA_QPAD:(h + 1) * MLA_QPAD])

    def shifted_logits(h, st):
        m = m_ref[h]
        m_new = jnp.maximum(m, jnp.max(st, axis=0, keepdims=True))
        m_ref[h] = m_new
        return jnp.exp2(m - m_new), (st - m_new).astype(BF16)

    def accumulate(off, h, alpha, d):
        acc_ref[h] = alpha * acc_ref[h] + _dot(
            vt_ref[0, h * MLA_VROWS:(h + 1) * MLA_VROWS, pl.ds(off, tk)], jnp.exp2(d))

    def key_tile(off, off_next, mask):
        pending = {0: st0_ref[...]}
        todo = list(range(1, nh)) + ([None] if off_next is not None else [])

        def issue():
            if todo:
                h = todo.pop(0)
                if h is None:
                    st0_ref[...] = scores(off_next, 0)
                else:
                    pending[h] = scores(off, h)

        for _ in range(ATT_LOOKAHEAD):
            issue()
        for h in range(nh):
            st = pending.pop(h)
            if mask is not None:
                st = jnp.where(mask, st, -jnp.inf)
            alpha, d = shifted_logits(h, st)
            issue()
            accumulate(off, h, alpha, d)

    st0_ref[...] = scores(0, 0)

    def body(j, carry):
        key_tile(pl.multiple_of(j * tk, tk), pl.multiple_of((j + 1) * tk, tk), None)
        return carry

    lax.fori_loop(0, qi, body, 0)
    key_tile(pl.multiple_of(qi * tk, tk), None, diag_mask)
    for h in range(nh):
        acc = acc_ref[h]
        ot_ref[h * MLA_V:(h + 1) * MLA_V, :] = (acc[:MLA_V] / acc[MLA_V:MLA_V + 1]).astype(BF16)
    o_ref[0] = h_ref[0] + _dot_tn(ot_ref[...], wo_ref[...])


def _mla_attn(q, kn, kr, vt, w_out, h3):
    bsz, seq, d = h3.shape
    tq = ATT_TQ
    nh = MLA_HEADS
    full = lambda n: pl.BlockSpec((1, seq, n), lambda b, i: (b, 0, 0), pipeline_mode=pl.Buffered(1))
    return pl.pallas_call(
        _attn_kernel,
        grid=(bsz, seq // tq),
        in_specs=[pl.BlockSpec((1, tq, nh * MLA_QPAD), lambda b, i: (b, i, 0)),
                  full(nh * MLA_NOPE), full(LANES),
                  pl.BlockSpec((1, nh * MLA_VROWS, seq), lambda b, i: (b, 0, 0),
                               pipeline_mode=pl.Buffered(1)),
                  _const_spec(w_out.shape),
                  pl.BlockSpec((1, tq, d), lambda b, i: (b, i, 0))],
        out_specs=pl.BlockSpec((1, tq, d), lambda b, i: (b, i, 0)),
        out_shape=jax.ShapeDtypeStruct((bsz, seq, d), F32),
        scratch_shapes=[pltpu.VMEM((nh * MLA_V, tq), BF16),
                        pltpu.VMEM((nh, 1, tq), F32),
                        pltpu.VMEM((nh, MLA_VROWS, tq), F32),
                        pltpu.VMEM((ATT_TK, tq), F32)],
        compiler_params=_params(2),
        name="mla_attn",
    )(q, kn, kr, vt, w_out, h3)


def _mla_weights(w_in, w_uq, w_ukv):
    nh, r = MLA_HEADS, MLA_ROPE
    qk = MLA_NOPE + MLA_ROPE

    def swap(w):
        return jnp.concatenate([-w[..., r // 2:], w[..., :r // 2]], axis=-1)

    k_rope = w_in[:, MLA_Q_RANK + MLA_KV_RANK:]
    w_in2 = jnp.concatenate([w_in, swap(k_rope)], axis=1)
    wq = w_uq.reshape(MLA_Q_RANK, nh, qk) * (qk ** -0.5 * math.log2(math.e))
    wq2 = jnp.concatenate([wq, swap(wq[..., MLA_NOPE:])], axis=-1).reshape(MLA_Q_RANK, nh * MLA_QPAD)
    wkv = w_ukv.reshape(MLA_KV_RANK, nh, 2, MLA_NOPE)
    w_ukn = wkv[:, :, 0, :].reshape(MLA_KV_RANK, nh * MLA_NOPE)
    w_uvt = jnp.pad(wkv[:, :, 1, :], ((0, 0), (0, 0), (0, MLA_VROWS - MLA_V)))
    w_uvt = w_uvt.reshape(MLA_KV_RANK, nh * MLA_VROWS).T
    return w_in2.astype(BF16), wq2.astype(BF16), w_ukn.astype(BF16), w_uvt.astype(BF16)


def _rope_table(positions):
    r = MLA_ROPE
    freqs = ROPE_THETA ** (-jnp.arange(0, r, 2, dtype=F32) / r)
    ang = positions.astype(F32).reshape(-1, 1) * freqs
    flat = ang.reshape(-1, LANES)
    cos = jnp.cos(flat).reshape(-1, r // 2)
    sin = jnp.sin(flat).reshape(-1, r // 2)
    return jnp.concatenate([cos, cos, sin, sin], axis=-1)


def kernel(x, positions, norm_mix_g, norm_ffn_g, final_norm_g, even_w_in, hgrn_lb_logits, hgrn_norm_g, s5_a_re, s5_a_im, s5_log_dt, s5_b_re, s5_b_im, s5_c_re, s5_c_im, s5_d, s5_w_glu, s5_b_glu, even_w_out, odd_w_in, mla_q_norm_g, mla_w_uq, mla_kv_norm_g, mla_w_ukv, odd_w_out, ffn_w_in, ffn_conv_w, ffn_conv_b, ffn_w_out):
    bsz, seq, d = x.shape
    t = bsz * seq
    depth = norm_mix_g.shape[0]
    lower_bounds = jnp.cumsum(jax.nn.softmax(hgrn_lb_logits.astype(F32), axis=0), axis=0)
    hgrn_dim = HGRN_HEADS * HGRN_HEAD_DIM
    h = x.reshape(t, d)
    ffn_w_in_bf, ffn_w_out_bf = ffn_w_in.astype(BF16), ffn_w_out.astype(BF16)
    for layer in range(depth):
        j = layer // 2
        mix = ()
        if layer % 2 == 0:
            proj = _norm_proj(h, norm_mix_g[layer], even_w_in[j].astype(BF16))
            proj3 = proj.reshape(bsz, seq, -1)
            ya = _hgrn(proj3, lower_bounds[j], hgrn_norm_g[j])
            s5_dim = s5_d.shape[1]
            tables = _s5_tables(s5_a_re[j], s5_a_im[j], s5_log_dt[j], s5_b_re[j], s5_b_im[j],
                                s5_c_re[j], s5_c_im[j], s5_d[j])
            yb = _s5(proj3, (4 * hgrn_dim) // s5_dim, tables, s5_w_glu[j], s5_b_glu[j])
            mix = ((ya.reshape(t, -1), even_w_out[j][:hgrn_dim]),
                   (yb.reshape(t, -1), even_w_out[j][hgrn_dim:]))
        else:
            w_in2, wq2, w_ukn, w_uvt = _mla_weights(odd_w_in[j], mla_w_uq[j], mla_w_ukv[j])
            q, kn, kr, vt = _mla_proj(h, seq, norm_mix_g[layer], w_in2, mla_q_norm_g[j], wq2,
                                      mla_kv_norm_g[j], w_ukn, w_uvt, _rope_table(positions))
            r3 = lambda a: a.reshape(bsz, seq, -1)
            h = _mla_attn(r3(q), r3(kn), r3(kr), vt, odd_w_out[j].astype(BF16),
                          h.reshape(bsz, seq, d)).reshape(t, d)
        last = layer == depth - 1
        h = _ffn(h, seq, layer, norm_ffn_g[layer], ffn_w_in_bf, ffn_conv_w[layer], ffn_conv_b[layer],
                 ffn_w_out_bf, final_norm_g if last else None, mix)
    return h.reshape(bsz, seq, d)
```

```python
import functools
import math

import jax
import jax.numpy as jnp
from jax import lax
from jax.experimental import pallas as pl
from jax.experimental.pallas import tpu as pltpu

F32 = jnp.float32
BF16 = jnp.bfloat16

EPS = 1e-6
HGRN_HEADS = 4
HGRN_HEAD_DIM = 128
HGRN_CHUNK = 64
S5_GROUP = 16
S5_STATE = 64
MLA_HEADS = 8
MLA_Q_RANK = 384
MLA_KV_RANK = 256
MLA_NOPE = 128
MLA_ROPE = 64
MLA_V = 128
ROPE_THETA = 10000.0
LANES = 128
MLA_QPAD = 2 * LANES
BF16_SUBLANES = 16
MLA_VROWS = MLA_V + BF16_SUBLANES

VMEM_LIMIT = 56 * 1024 * 1024


def _params(n_axes, vmem=VMEM_LIMIT):
    return pltpu.CompilerParams(dimension_semantics=("arbitrary",) * n_axes,
                                vmem_limit_bytes=vmem)


def _const_spec(shape):
    zeros = (0,) * len(shape)
    return pl.BlockSpec(shape, lambda *_: zeros, pipeline_mode=pl.Buffered(1))


def _rms(x, g):
    return x * lax.rsqrt(jnp.mean(x * x, axis=-1, keepdims=True) + EPS) * g


def _sigmoid(x):
    return 0.5 * jnp.tanh(0.5 * x) + 0.5


def _dot(a, b):
    return jnp.dot(a, b, preferred_element_type=F32)


def _dot_nt(a, b):
    return lax.dot_general(a, b, (((1,), (1,)), ((), ())), preferred_element_type=F32)


def _dot_tn(a, b):
    return lax.dot_general(a, b, (((0,), (0,)), ((), ())), preferred_element_type=F32)


def _norm_proj_kernel(h_ref, g_ref, w_ref, o_ref):
    hn = _rms(h_ref[...], g_ref[...]).astype(BF16)
    o_ref[...] = _dot(hn, w_ref[...]).astype(o_ref.dtype)


def _norm_proj(h, g, w, tm=512):
    t, d = h.shape
    n = w.shape[1]
    return pl.pallas_call(
        _norm_proj_kernel,
        grid=(t // tm,),
        in_specs=[pl.BlockSpec((tm, d), lambda i: (i, 0)),
                  _const_spec((1, d)),
                  _const_spec((d, n))],
        out_specs=pl.BlockSpec((tm, n), lambda i: (i, 0)),
        out_shape=jax.ShapeDtypeStruct((t, n), F32),
        compiler_params=_params(1),
        name="even_in",
    )(h, g.reshape(1, d), w)


HGRN_TILE = 512
HGRN_ATT_BLOCK = 256


def _hgrn_kernel(q_ref, f_ref, i_ref, g_ref, lb_ref, ng_ref, o_ref, st_ref):
    L, C, AB = HGRN_TILE, HGRN_CHUNK, HGRN_ATT_BLOCK
    dh = HGRN_HEAD_DIM

    nh = HGRN_HEADS

    @pl.when(pl.program_id(1) == 0)
    def _():
        st_ref[...] = jnp.zeros_like(st_ref)

    q = q_ref[0]
    f = f_ref[0]
    g = g_ref[0]
    lb = lb_ref[...]
    forget = lb + (1.0 - lb) * _sigmoid(f)
    kh = 1.0 - forget
    ri = lax.broadcasted_iota(jnp.int32, (AB, AB), 0)
    ci = lax.broadcasted_iota(jnp.int32, (AB, AB), 1)
    mask = jnp.logical_and((ri // C) == (ci // C), ci <= ri)

    logf = jnp.log(forget)
    lf_hi = logf.astype(BF16)
    lf_lo = (logf - lf_hi.astype(F32)).astype(BF16)
    tri = jnp.where(mask, 1.0, 0.0).astype(BF16)
    b = jnp.concatenate(
        [_dot(tri, lf_hi[r0:r0 + AB]) + _dot(tri, lf_lo[r0:r0 + AB]) for r0 in range(0, L, AB)],
        axis=0)
    qd = (q * jnp.exp(b)).astype(BF16)
    kd = (kh * jnp.exp(-b)).astype(BF16)
    vb = i_ref[0].astype(BF16)
    nc = L // C
    chunk = [slice(c * C, (c + 1) * C) for c in range(nc)]
    head = [slice(h * dh, (h + 1) * dh) for h in range(nh)]
    bl = [b[c * C + C - 1:(c + 1) * C, :] for c in range(nc)]
    kdec = (kh * jnp.exp(jnp.concatenate(
        [jnp.broadcast_to(bl[c], (C, nh * dh)) for c in range(nc)], axis=0) - b)).astype(BF16)
    decay = [jnp.exp(x) for x in bl]
    ds = [[_dot_tn(vb[chunk[c], head[h]], kdec[chunk[c], head[h]]) for h in range(nh)]
          for c in range(nc)]
    st = [st_ref[h] for h in range(nh)]
    st_in = []
    for c in range(nc):
        st_in.append([s.astype(BF16) for s in st])
        st = [st[h] * decay[c][:, head[h]] + ds[c][h] for h in range(nh)]
    for h in range(nh):
        st_ref[h] = st[h]
    block = [slice(k * AB, (k + 1) * AB) for k in range(L // AB)]
    att = [[_dot_nt(qd[r, head[h]], kd[r, head[h]]) for h in range(nh)] for r in block]
    att = [[jnp.where(mask, a, 0.0).astype(BF16) for a in row] for row in att]
    o_intra = [[_dot(att[k][h], vb[block[k], head[h]]) for h in range(nh)]
               for k in range(len(block))]
    o_inter = [[_dot_nt(qd[chunk[c], head[h]], st_in[c][h]) for h in range(nh)] for c in range(nc)]
    outs = []
    for h in range(nh):
        o = (jnp.concatenate([row[h] for row in o_intra], axis=0)
             + jnp.concatenate([row[h] for row in o_inter], axis=0))
        outs.append(o * lax.rsqrt(jnp.mean(o * o, axis=-1, keepdims=True) + EPS))
    o = jnp.concatenate(outs, axis=1) * ng_ref[...]
    o_ref[0] = (o * (g * _sigmoid(g))).astype(o_ref.dtype)


def _hgrn(proj3, lb, norm_g):
    bsz, seq, _ = proj3.shape
    nh, dh, L = HGRN_HEADS, HGRN_HEAD_DIM, HGRN_TILE
    w = nh * dh

    def col(k):
        return pl.BlockSpec((1, L, w), lambda b, s, k=k: (b, s, k))

    return pl.pallas_call(
        _hgrn_kernel,
        grid=(bsz, seq // L),
        in_specs=[col(0), col(1), col(2), col(3), _const_spec((1, w)), _const_spec((1, w))],
        out_specs=pl.BlockSpec((1, L, w), lambda b, s: (b, s, 0)),
        out_shape=jax.ShapeDtypeStruct((bsz, seq, w), BF16),
        scratch_shapes=[pltpu.VMEM((nh, dh, dh), F32)],
        compiler_params=_params(2),
        name="hgrn2",
    )(proj3, proj3, proj3, proj3, lb.reshape(1, w), norm_g.reshape(1, w))


S5_LC = 16
S5_RC = 16
LANE_BLOCKS = LANES // S5_GROUP


def _s5_tables(a_re, a_im, log_dt, b_re, b_im, c_re, c_im, d_skip):
    G, P = a_re.shape
    Hc = b_re.shape[-1]
    Lc = S5_LC
    dt = jnp.exp(log_dt)[:, None]
    lam, th = a_re * dt, a_im * dt
    mag = jnp.exp(lam)
    abar_re, abar_im = mag * jnp.cos(th), mag * jnp.sin(th)
    den = a_re * a_re + a_im * a_im
    xr, xi = abar_re - 1.0, abar_im
    coef_re = ((xr * a_re + xi * a_im) / den)[..., None]
    coef_im = ((xi * a_re - xr * a_im) / den)[..., None]
    bb_re = coef_re * b_re - coef_im * b_im
    bb_im = coef_re * b_im + coef_im * b_re
    n = jnp.arange(Lc + 1, dtype=F32)[:, None, None]
    pw_re = jnp.exp(n * lam) * jnp.cos(n * th)
    pw_im = jnp.exp(n * lam) * jnp.sin(n * th)
    bt_re, bt_im = bb_re.transpose(0, 2, 1), bb_im.transpose(0, 2, 1)
    cp_re = c_re[None] * pw_re[:Lc, :, None, :] - c_im[None] * pw_im[:Lc, :, None, :]
    cp_im = c_re[None] * pw_im[:Lc, :, None, :] + c_im[None] * pw_re[:Lc, :, None, :]
    kern = jnp.sum(cp_re[:, :, :, None, :] * bt_re[None, :, None, :, :]
                   - cp_im[:, :, :, None, :] * bt_im[None, :, None, :, :], axis=-1)
    skip = d_skip.reshape(G, Hc)[:, :, None] * jnp.eye(Hc, dtype=F32)
    kern = kern.at[0].add(skip)
    kcat = kern.transpose(1, 3, 0, 2).reshape(G, Hc, Lc * Hc)
    pr = pw_re[:Lc][::-1].transpose(1, 0, 2)[:, :, None, :]
    pi = pw_im[:Lc][::-1].transpose(1, 0, 2)[:, :, None, :]
    bs_re = pr * bt_re[:, None] - pi * bt_im[:, None]
    bs_im = pr * bt_im[:, None] + pi * bt_re[:, None]
    bs = jnp.concatenate([bs_re, bs_im], axis=-1).reshape(G, Lc * Hc, 2 * P)
    bs_twin = jnp.concatenate([bs_im, bs_re], axis=-1).reshape(G, Lc * Hc, 2 * P)
    ct_re = c_re.transpose(0, 2, 1)[:, :, None, :]
    ct_im = c_im.transpose(0, 2, 1)[:, :, None, :]
    qr = pw_re[1:].transpose(1, 2, 0)[..., None]
    qi = pw_im[1:].transpose(1, 2, 0)[..., None]
    cs = jnp.concatenate([ct_re * qr - ct_im * qi, -(ct_re * qi + ct_im * qr)],
                         axis=1).reshape(G, 2 * P, Lc * Hc)
    bsw = jnp.concatenate([bs, bs_twin], axis=2).astype(BF16)
    a_n_re, a_n_im = pw_re[Lc], pw_im[Lc]
    m1 = jnp.concatenate([a_n_re, a_n_re], axis=1).reshape(G, 1, 2 * P)
    m2 = jnp.concatenate([-a_n_im, a_n_im], axis=1).reshape(G, 1, 2 * P)
    return kcat, bsw, cs.astype(BF16), m1, m2


def _gelu_tanh(x):
    return 0.5 * x * (1.0 + jnp.tanh(math.sqrt(2.0 / math.pi) * (x + 0.044715 * (x * x * x))))


def _transpose_lane_blocks(vs, blk):
    vs = list(vs)
    d = LANE_BLOCKS // 2
    while d >= 1:
        low = (blk & d) == 0
        for i in range(LANE_BLOCKS):
            if i & d == 0:
                a, b = vs[i], vs[i + d]
                vs[i] = jnp.where(low, a, pltpu.roll(b, d * S5_GROUP, axis=1))
                vs[i + d] = jnp.where(low, pltpu.roll(a, LANES - d * S5_GROUP, axis=1), b)
        d //= 2
    return vs


def _s5_kernel(*refs):
    nq = len(refs) - 15
    u_refs = refs[:nq]
    (kcat_ref, bsw_ref, cs_ref, m1_ref, m2_ref, wglu_ref, bglu_ref, o_ref,
     h_ref, toep_ref, x_ref, y_ref, dh_ref, hs_ref, yt_ref) = refs[nq:]
    nb, tok, _ = u_refs[0].shape
    Lc = S5_LC
    RC = tok // Lc
    R = nb * RC
    G = bsw_ref.shape[0]
    gw = Lc * S5_GROUP
    nhalf = Lc // LANE_BLOCKS

    @pl.when(pl.program_id(0) == 0)
    def _():
        h_ref[...] = jnp.zeros_like(h_ref)
        lane = lax.broadcasted_iota(jnp.int32, (S5_GROUP, gw), 1)

        def build(g, carry):
            kc = kcat_ref[g]
            for s in range(Lc):
                rows = kc if s == 0 else jnp.where(
                    lane >= s * S5_GROUP, pltpu.roll(kc, s * S5_GROUP, axis=1), 0.0)
                toep_ref[g, s * S5_GROUP:(s + 1) * S5_GROUP, :] = rows.astype(BF16)
            return carry

        lax.fori_loop(0, G, build, 0)

    blk = lax.broadcasted_iota(jnp.int32, (R, LANES), 1) // S5_GROUP

    for half in range(nhalf):
        for vq in range(nq):
            ws = _transpose_lane_blocks(
                [u_refs[vq][:, pl.ds(half * LANE_BLOCKS + tl, RC, stride=Lc), :].reshape(R, LANES)
                 for tl in range(LANE_BLOCKS)], blk)
            for gl in range(LANE_BLOCKS):
                col = (nhalf * (LANE_BLOCKS * vq + gl) + half) * LANES
                x_ref[:, col:col + LANES] = ws[gl].astype(BF16)

    sw = 2 * S5_STATE
    for g in range(G):
        xg = x_ref[:, g * gw:(g + 1) * gw]
        y_ref[:, g * gw:(g + 1) * gw] = _dot(xg, toep_ref[g])
        r = _dot(xg, bsw_ref[g])
        dh_ref[0, g] = r[:, :sw]
        dh_ref[1, g] = r[:, sw:]

    hp = [h_ref[0, g] for g in range(G)]
    hq = [h_ref[1, g] for g in range(G)]
    for c in range(RC):
        rows = pl.ds(c, nb, stride=RC)
        for g in range(G):
            hs_ref[g, rows, :] = hp[g]
            m1, m2 = m1_ref[g], m2_ref[g]
            hp[g], hq[g] = (hp[g] * m1 + hq[g] * m2 + dh_ref[0, g, rows, :],
                            hq[g] * m1 - hp[g] * m2 + dh_ref[1, g, rows, :])
    for g in range(G):
        h_ref[0, g] = hp[g]
        h_ref[1, g] = hq[g]

    for g in range(G):
        y_ref[:, g * gw:(g + 1) * gw] += _dot(hs_ref[g].astype(BF16), cs_ref[g])

    for half in range(nhalf):
        for vq in range(nq):
            cols = [(nhalf * (LANE_BLOCKS * vq + gl) + half) * LANES for gl in range(LANE_BLOCKS)]
            ws = _transpose_lane_blocks([y_ref[:, c0:c0 + LANES] for c0 in cols], blk)
            for tl in range(LANE_BLOCKS):
                yt_ref[vq, :, pl.ds(half * LANE_BLOCKS + tl, RC, stride=Lc), :] = (
                    ws[tl].reshape(nb, RC, LANES))

    for b in range(nb):
        z = _gelu_tanh(jnp.concatenate([yt_ref[vq, b] for vq in range(nq)], axis=1))
        gate = _sigmoid(_dot(z.astype(BF16), wglu_ref[...]) + bglu_ref[...])
        o_ref[b] = (z * gate).astype(o_ref.dtype)


def _s5(proj3, u_col_block, tables, w_glu, b_glu):
    bsz, seq, _ = proj3.shape
    kcat, bsw, cs, m1, m2 = tables
    dim = w_glu.shape[0]
    tok = S5_RC * S5_LC
    rows = bsz * S5_RC
    G, _, sw = m1.shape
    gw = kcat.shape[2]
    nq = dim // LANES
    q0 = u_col_block * nq
    u_specs = [pl.BlockSpec((bsz, tok, LANES), lambda i, q=q: (0, i, q0 + q)) for q in range(nq)]
    return pl.pallas_call(
        _s5_kernel,
        grid=(seq // tok,),
        in_specs=u_specs + [_const_spec(kcat.shape), _const_spec(bsw.shape), _const_spec(cs.shape),
                            _const_spec(m1.shape), _const_spec(m2.shape),
                            _const_spec(w_glu.shape), _const_spec((1, dim))],
        out_specs=pl.BlockSpec((bsz, tok, dim), lambda i: (0, i, 0)),
        out_shape=jax.ShapeDtypeStruct((bsz, seq, dim), BF16),
        scratch_shapes=[pltpu.VMEM((2, G, bsz, sw), F32),
                        pltpu.VMEM((G, gw, gw), BF16),
                        pltpu.VMEM((rows, S5_LC * dim), BF16),
                        pltpu.VMEM((rows, S5_LC * dim), F32),
                        pltpu.VMEM((2, G, rows, sw), F32),
                        pltpu.VMEM((G, rows, sw), F32),
                        pltpu.VMEM((nq, bsz, tok, LANES), F32)],
        compiler_params=_params(1),
        name="s5",
    )(*([proj3] * nq), kcat, bsw, cs, m1, m2, w_glu.astype(BF16), b_glu.reshape(1, dim))


FFN_TILE = 512
FFN_CHUNK = 256
SUBLANES = 8


def _ffn_kernel(tiles_per_seq, final_norm, n_mix, *refs):
    mix = refs[:2 * n_mix]
    (h_ref, g_ref, wa_ref, wu_ref, cw_ref, cb_ref, wo_ref, fg_ref, o_ref,
     prev_ref, abuf_ref, act_ref) = refs[2 * n_mix:]
    tm, fc, hal = FFN_TILE, FFN_CHUNK, SUBLANES
    dff = wa_ref.shape[1]

    @pl.when(pl.program_id(0) % tiles_per_seq == 0)
    def _():
        prev_ref[...] = jnp.zeros_like(prev_ref)

    x = h_ref[...]
    for k in range(n_mix):
        x = x + _dot(mix[2 * k][...], mix[2 * k + 1][...])
    hn = _rms(x, g_ref[...]).astype(BF16)
    for c in range(dff // fc):
        cols = slice(c * fc, (c + 1) * fc)
        a = _dot(hn, wa_ref[:, cols])
        u = _dot(hn, wu_ref[:, cols])
        ab = abuf_ref.at[c % 2]
        ab[0:hal, :] = prev_ref[:, cols]
        ab[hal:hal + tm, :] = a
        prev_ref[:, cols] = a[tm - hal:tm, :]
        a1 = ab[hal - 1:hal - 1 + tm, :]
        a2 = ab[hal - 2:hal - 2 + tm, :]
        w = cw_ref[:, cols]
        conv = a * w[2:3, :] + a1 * w[1:2, :] + a2 * w[0:1, :] + cb_ref[:, cols]
        act_ref[:, cols] = (conv * _sigmoid(conv) * u).astype(BF16)
    y = x + _dot(act_ref[...], wo_ref[...])
    if final_norm:
        y = _rms(y, fg_ref[...])
    o_ref[...] = y


def _ffn(h, seq, layer, g, w_in, conv_w, conv_b, w_out, final_g=None, mix=()):
    t, d = h.shape
    dff = w_out.shape[1]
    tm, fc, hal = FFN_TILE, FFN_CHUNK, SUBLANES
    final_norm = final_g is not None
    fg = final_g if final_norm else g
    mix_specs, mix_args = [], []
    for y, w in mix:
        mix_specs += [pl.BlockSpec((tm, y.shape[1]), lambda i: (i, 0)), _const_spec(w.shape)]
        mix_args += [y, w.astype(BF16)]
    return pl.pallas_call(
        functools.partial(_ffn_kernel, seq // tm, final_norm, len(mix)),
        grid=(t // tm,),
        in_specs=mix_specs + [
                  pl.BlockSpec((tm, d), lambda i: (i, 0)),
                  _const_spec((1, d)),
                  pl.BlockSpec((None, d, dff), lambda i: (layer, 0, 0), pipeline_mode=pl.Buffered(1)),
                  pl.BlockSpec((None, d, dff), lambda i: (layer, 0, 1), pipeline_mode=pl.Buffered(1)),
                  _const_spec((conv_w.shape[0], dff)), _const_spec((1, dff)),
                  pl.BlockSpec((None, dff, d), lambda i: (layer, 0, 0), pipeline_mode=pl.Buffered(1)),
                  _const_spec((1, d))],
        out_specs=pl.BlockSpec((tm, d), lambda i: (i, 0)),
        out_shape=jax.ShapeDtypeStruct((t, d), F32),
        scratch_shapes=[pltpu.VMEM((hal, dff), F32),
                        pltpu.VMEM((2, hal + tm, fc), F32),
                        pltpu.VMEM((tm, dff), BF16)],
        compiler_params=_params(1),
        name="ffn",
    )(*mix_args, h, g.reshape(1, d), w_in, w_in, conv_w, conv_b.reshape(1, dff),
      w_out, fg.reshape(1, d))


MLA_TILE = 512


def _mla_proj_kernel(h_ref, g_ref, win_ref, qg_ref, wuq_ref, kvg_ref, wukn_ref, wuvt_ref,
                     rope_ref, q_ref, kn_ref, kr_ref, vt_ref):
    nh = MLA_HEADS
    hn = _rms(h_ref[...], g_ref[...]).astype(BF16)
    proj = _dot(hn, win_ref[...])
    cq = _rms(proj[:, :MLA_Q_RANK], qg_ref[...]).astype(BF16)
    ckv = _rms(proj[:, MLA_Q_RANK:MLA_Q_RANK + MLA_KV_RANK], kvg_ref[...]).astype(BF16)
    tab = rope_ref[...]

    def rope(x):
        y = x * tab
        return y + pltpu.roll(y, MLA_ROPE, axis=1)

    lane = lax.broadcasted_iota(jnp.int32, tab.shape, 1)
    kr = jnp.where(lane < MLA_ROPE, rope(proj[:, MLA_Q_RANK + MLA_KV_RANK:]), 0.0)
    kr_ref[...] = kr.astype(kr_ref.dtype)
    qf = _dot(cq, wuq_ref[...])
    pieces = []
    for h in range(nh):
        lo = h * MLA_QPAD
        pieces.append(qf[:, lo:lo + LANES])
        pieces.append(rope(qf[:, lo + LANES:lo + MLA_QPAD]))
    q_ref[...] = jnp.concatenate(pieces, axis=1).astype(q_ref.dtype)
    kn_ref[...] = _dot(ckv, wukn_ref[...]).astype(kn_ref.dtype)
    vt = _dot_nt(wuvt_ref[...], ckv)
    row = lax.broadcasted_iota(jnp.int32, vt.shape, 0)
    vt_ref[0] = jnp.where(row % MLA_VROWS >= MLA_V, 1.0, vt).astype(vt_ref.dtype)


def _mla_proj(h, seq, g, w_in, q_norm_g, w_uq, kv_norm_g, w_ukn, w_uvt, rope_tab):
    t, d = h.shape
    tm, nh = MLA_TILE, MLA_HEADS
    tps = seq // tm
    row = lambda n: pl.BlockSpec((tm, n), lambda i: (i, 0))
    return pl.pallas_call(
        _mla_proj_kernel,
        grid=(t // tm,),
        in_specs=[row(d), _const_spec((1, d)), _const_spec(w_in.shape),
                  _const_spec((1, MLA_Q_RANK)), _const_spec(w_uq.shape),
                  _const_spec((1, MLA_KV_RANK)), _const_spec(w_ukn.shape), _const_spec(w_uvt.shape),
                  row(LANES)],
        out_specs=[row(nh * MLA_QPAD), row(nh * MLA_NOPE), row(LANES),
                   pl.BlockSpec((1, nh * MLA_VROWS, tm), lambda i: (i // tps, 0, i % tps))],
        out_shape=[jax.ShapeDtypeStruct((t, nh * MLA_QPAD), BF16),
                   jax.ShapeDtypeStruct((t, nh * MLA_NOPE), BF16),
                   jax.ShapeDtypeStruct((t, LANES), BF16),
                   jax.ShapeDtypeStruct((t // seq, nh * MLA_VROWS, seq), BF16)],
        compiler_params=_params(1),
        name="mla_proj",
    )(h, g.reshape(1, d), w_in, q_norm_g.reshape(1, -1), w_uq, kv_norm_g.reshape(1, -1),
      w_ukn, w_uvt, rope_tab)


ATT_TQ = 512
ATT_TK = 512
ATT_LOOKAHEAD = 0


def _attn_kernel(q_ref, kn_ref, kr_ref, vt_ref, wo_ref, h_ref, o_ref,
                 ot_ref, m_ref, acc_ref, st0_ref):
    tq, tk = ATT_TQ, ATT_TK
    qi = pl.program_id(1)
    ki = lax.broadcasted_iota(jnp.int32, (tk, tq), 0)
    qc = lax.broadcasted_iota(jnp.int32, (tk, tq), 1)
    diag_mask = ki <= qc

    nh = MLA_HEADS
    m_ref[...] = jnp.full(m_ref.shape, -jnp.inf, F32)
    acc_ref[...] = jnp.zeros(acc_ref.shape, F32)

    def scores(off, h):
        rows = pl.ds(off, tk)
        kj = jnp.concatenate([kn_ref[0, rows, h * MLA_NOPE:(h + 1) * MLA_NOPE], kr_ref[0, rows, :]],
                             axis=1)
        return _dot_nt(kj, q_ref[0, :, h * MLA_QPAD:(h + 1) * MLA_QPAD])

    def shifted_logits(h, st):
        m = m_ref[h]
        m_new = jnp.maximum(m, jnp.max(st, axis=0, keepdims=True))
        m_ref[h] = m_new
        return jnp.exp2(m - m_new), (st - m_new).astype(BF16)

    def accumulate(off, h, alpha, d):
        acc_ref[h] = alpha * acc_ref[h] + _dot(
            vt_ref[0, h * MLA_VROWS:(h + 1) * MLA_VROWS, pl.ds(off, tk)], jnp.exp2(d))

    def key_tile(off, off_next, mask):
        pending = {0: st0_ref[...]}
        todo = list(range(1, nh)) + ([None] if off_next is not None else [])

        def issue():
            if todo:
                h = todo.pop(0)
                if h is None:
                    st0_ref[...] = scores(off_next, 0)
                else:
                    pending[h] = scores(off, h)

        for _ in range(ATT_LOOKAHEAD):
            issue()
        for h in range(nh):
            st = pending.pop(h)
            if mask is not None:
                st = jnp.where(mask, st, -jnp.inf)
            alpha, d = shifted_logits(h, st)
            issue()
            accumulate(off, h, alpha, d)

    st0_ref[...] = scores(0, 0)

    def body(j, carry):
        key_tile(pl.multiple_of(j * tk, tk), pl.multiple_of((j + 1) * tk, tk), None)
        return carry

    lax.fori_loop(0, qi, body, 0)
    key_tile(pl.multiple_of(qi * tk, tk), None, diag_mask)
    for h in range(nh):
        acc = acc_ref[h]
        ot_ref[h * MLA_V:(h + 1) * MLA_V, :] = (acc[:MLA_V] / acc[MLA_V:MLA_V + 1]).astype(BF16)
    o_ref[0] = h_ref[0] + _dot_tn(ot_ref[...], wo_ref[...])


def _mla_attn(q, kn, kr, vt, w_out, h3):
    bsz, seq, d = h3.shape
    tq = ATT_TQ
    nh = MLA_HEADS
    full = lambda n: pl.BlockSpec((1, seq, n), lambda b, i: (b, 0, 0), pipeline_mode=pl.Buffered(1))
    return pl.pallas_call(
        _attn_kernel,
        grid=(bsz, seq // tq),
        in_specs=[pl.BlockSpec((1, tq, nh * MLA_QPAD), lambda b, i: (b, i, 0)),
                  full(nh * MLA_NOPE), full(LANES),
                  pl.BlockSpec((1, nh * MLA_VROWS, seq), lambda b, i: (b, 0, 0),
                               pipeline_mode=pl.Buffered(1)),
                  _const_spec(w_out.shape),
                  pl.BlockSpec((1, tq, d), lambda b, i: (b, i, 0))],
        out_specs=pl.BlockSpec((1, tq, d), lambda b, i: (b, i, 0)),
        out_shape=jax.ShapeDtypeStruct((bsz, seq, d), F32),
        scratch_shapes=[pltpu.VMEM((nh * MLA_V, tq), BF16),
                        pltpu.VMEM((nh, 1, tq), F32),
                        pltpu.VMEM((nh, MLA_VROWS, tq), F32),
                        pltpu.VMEM((ATT_TK, tq), F32)],
        compiler_params=_params(2),
        name="mla_attn",
    )(q, kn, kr, vt, w_out, h3)


def _mla_weights(w_in, w_uq, w_ukv):
    nh, r = MLA_HEADS, MLA_ROPE
    qk = MLA_NOPE + MLA_ROPE

    def swap(w):
        return jnp.concatenate([-w[..., r // 2:], w[..., :r // 2]], axis=-1)

    k_rope = w_in[:, MLA_Q_RANK + MLA_KV_RANK:]
    w_in2 = jnp.concatenate([w_in, swap(k_rope)], axis=1)
    wq = w_uq.reshape(MLA_Q_RANK, nh, qk) * (qk ** -0.5 * math.log2(math.e))
    wq2 = jnp.concatenate([wq, swap(wq[..., MLA_NOPE:])], axis=-1).reshape(MLA_Q_RANK, nh * MLA_QPAD)
    wkv = w_ukv.reshape(MLA_KV_RANK, nh, 2, MLA_NOPE)
    w_ukn = wkv[:, :, 0, :].reshape(MLA_KV_RANK, nh * MLA_NOPE)
    w_uvt = jnp.pad(wkv[:, :, 1, :], ((0, 0), (0, 0), (0, MLA_VROWS - MLA_V)))
    w_uvt = w_uvt.reshape(MLA_KV_RANK, nh * MLA_VROWS).T
    return w_in2.astype(BF16), wq2.astype(BF16), w_ukn.astype(BF16), w_uvt.astype(BF16)


def _rope_table(positions):
    r = MLA_ROPE
    freqs = ROPE_THETA ** (-jnp.arange(0, r, 2, dtype=F32) / r)
    ang = positions.astype(F32).reshape(-1, 1) * freqs
    flat = ang.reshape(-1, LANES)
    cos = jnp.cos(flat).reshape(-1, r // 2)
    sin = jnp.sin(flat).reshape(-1, r // 2)
    return jnp.concatenate([cos, cos, sin, sin], axis=-1)


def kernel(x, positions, norm_mix_g, norm_ffn_g, final_norm_g, even_w_in, hgrn_lb_logits, hgrn_norm_g, s5_a_re, s5_a_im, s5_log_dt, s5_b_re, s5_b_im, s5_c_re, s5_c_im, s5_d, s5_w_glu, s5_b_glu, even_w_out, odd_w_in, mla_q_norm_g, mla_w_uq, mla_kv_norm_g, mla_w_ukv, odd_w_out, ffn_w_in, ffn_conv_w, ffn_conv_b, ffn_w_out):
    bsz, seq, d = x.shape
    t = bsz * seq
    depth = norm_mix_g.shape[0]
    lower_bounds = jnp.cumsum(jax.nn.softmax(hgrn_lb_logits.astype(F32), axis=0), axis=0)
    hgrn_dim = HGRN_HEADS * HGRN_HEAD_DIM
    h = x.reshape(t, d)
    ffn_w_in_bf, ffn_w_out_bf = ffn_w_in.astype(BF16), ffn_w_out.astype(BF16)
    for layer in range(depth):
        j = layer // 2
        mix = ()
        if layer % 2 == 0:
            proj = _norm_proj(h, norm_mix_g[layer], even_w_in[j].astype(BF16))
            proj3 = proj.reshape(bsz, seq, -1)
            ya = _hgrn(proj3, lower_bounds[j], hgrn_norm_g[j])
            s5_dim = s5_d.shape[1]
            tables = _s5_tables(s5_a_re[j], s5_a_im[j], s5_log_dt[j], s5_b_re[j], s5_b_im[j],
                                s5_c_re[j], s5_c_im[j], s5_d[j])
            yb = _s5(proj3, (4 * hgrn_dim) // s5_dim, tables, s5_w_glu[j], s5_b_glu[j])
            mix = ((ya.reshape(t, -1), even_w_out[j][:hgrn_dim]),
                   (yb.reshape(t, -1), even_w_out[j][hgrn_dim:]))
        else:
            w_in2, wq2, w_ukn, w_uvt = _mla_weights(odd_w_in[j], mla_w_uq[j], mla_w_ukv[j])
            q, kn, kr, vt = _mla_proj(h, seq, norm_mix_g[layer], w_in2, mla_q_norm_g[j], wq2,
                                      mla_kv_norm_g[j], w_ukn, w_uvt, _rope_table(positions))
            r3 = lambda a: a.reshape(bsz, seq, -1)
            h = _mla_attn(r3(q), r3(kn), r3(kr), vt, odd_w_out[j].astype(BF16),
                          h.reshape(bsz, seq, d)).reshape(t, d)
        last = layer == depth - 1
        h = _ffn(h, seq, layer, norm_ffn_g[layer], ffn_w_in_bf, ffn_conv_w[layer], ffn_conv_b[layer],
                 ffn_w_out_bf, final_norm_g if last else None, mix)
    return h.reshape(bsz, seq, d)
```

```python
import functools
import math

import jax
import jax.numpy as jnp
from jax import lax
from jax.experimental import pallas as pl
from jax.experimental.pallas import tpu as pltpu

F32 = jnp.float32
BF16 = jnp.bfloat16

EPS = 1e-6
HGRN_HEADS = 4
HGRN_HEAD_DIM = 128
HGRN_CHUNK = 64
S5_GROUP = 16
S5_STATE = 64
MLA_HEADS = 8
MLA_Q_RANK = 384
MLA_KV_RANK = 256
MLA_NOPE = 128
MLA_ROPE = 64
MLA_V = 128
ROPE_THETA = 10000.0
LANES = 128
MLA_QPAD = 2 * LANES
BF16_SUBLANES = 16
MLA_VROWS = MLA_V + BF16_SUBLANES

VMEM_LIMIT = 56 * 1024 * 1024


def _params(n_axes, vmem=VMEM_LIMIT):
    return pltpu.CompilerParams(dimension_semantics=("arbitrary",) * n_axes,
                                vmem_limit_bytes=vmem)


def _const_spec(shape):
    zeros = (0,) * len(shape)
    return pl.BlockSpec(shape, lambda *_: zeros, pipeline_mode=pl.Buffered(1))


def _rms(x, g):
    return x * lax.rsqrt(jnp.mean(x * x, axis=-1, keepdims=True) + EPS) * g


def _sigmoid(x):
    return 0.5 * jnp.tanh(0.5 * x) + 0.5


def _dot(a, b):
    return jnp.dot(a, b, preferred_element_type=F32)


def _dot_nt(a, b):
    return lax.dot_general(a, b, (((1,), (1,)), ((), ())), preferred_element_type=F32)


def _dot_tn(a, b):
    return lax.dot_general(a, b, (((0,), (0,)), ((), ())), preferred_element_type=F32)


def _norm_proj_kernel(h_ref, g_ref, w_ref, o_ref):
    hn = _rms(h_ref[...], g_ref[...]).astype(BF16)
    o_ref[...] = _dot(hn, w_ref[...]).astype(o_ref.dtype)


def _norm_proj(h, g, w, tm=512):
    t, d = h.shape
    n = w.shape[1]
    return pl.pallas_call(
        _norm_proj_kernel,
        grid=(t // tm,),
        in_specs=[pl.BlockSpec((tm, d), lambda i: (i, 0)),
                  _const_spec((1, d)),
                  _const_spec((d, n))],
        out_specs=pl.BlockSpec((tm, n), lambda i: (i, 0)),
        out_shape=jax.ShapeDtypeStruct((t, n), F32),
        compiler_params=_params(1),
        name="even_in",
    )(h, g.reshape(1, d), w)


HGRN_TILE = 512
HGRN_ATT_BLOCK = 256


def _hgrn_kernel(q_ref, f_ref, i_ref, g_ref, lb_ref, ng_ref, o_ref, st_ref):
    L, C, AB = HGRN_TILE, HGRN_CHUNK, HGRN_ATT_BLOCK
    dh = HGRN_HEAD_DIM

    nh = HGRN_HEADS

    @pl.when(pl.program_id(1) == 0)
    def _():
        st_ref[...] = jnp.zeros_like(st_ref)

    q = q_ref[0]
    f = f_ref[0]
    g = g_ref[0]
    lb = lb_ref[...]
    forget = lb + (1.0 - lb) * _sigmoid(f)
    kh = 1.0 - forget
    ri = lax.broadcasted_iota(jnp.int32, (AB, AB), 0)
    ci = lax.broadcasted_iota(jnp.int32, (AB, AB), 1)
    mask = jnp.logical_and((ri // C) == (ci // C), ci <= ri)

    logf = jnp.log(forget)
    lf_hi = logf.astype(BF16)
    lf_lo = (logf - lf_hi.astype(F32)).astype(BF16)
    tri = jnp.where(mask, 1.0, 0.0).astype(BF16)
    b = jnp.concatenate(
        [_dot(tri, lf_hi[r0:r0 + AB]) + _dot(tri, lf_lo[r0:r0 + AB]) for r0 in range(0, L, AB)],
        axis=0)
    qd = (q * jnp.exp(b)).astype(BF16)
    kd = (kh * jnp.exp(-b)).astype(BF16)
    vb = i_ref[0].astype(BF16)
    nc = L // C
    chunk = [slice(c * C, (c + 1) * C) for c in range(nc)]
    head = [slice(h * dh, (h + 1) * dh) for h in range(nh)]
    bl = [b[c * C + C - 1:(c + 1) * C, :] for c in range(nc)]
    kdec = (kh * jnp.exp(jnp.concatenate(
        [jnp.broadcast_to(bl[c], (C, nh * dh)) for c in range(nc)], axis=0) - b)).astype(BF16)
    decay = [jnp.exp(x) for x in bl]
    ds = [[_dot_tn(vb[chunk[c], head[h]], kdec[chunk[c], head[h]]) for h in range(nh)]
          for c in range(nc)]
    st = [st_ref[h] for h in range(nh)]
    st_in = []
    for c in range(nc):
        st_in.append([s.astype(BF16) for s in st])
        st = [st[h] * decay[c][:, head[h]] + ds[c][h] for h in range(nh)]
    for h in range(nh):
        st_ref[h] = st[h]
    block = [slice(k * AB, (k + 1) * AB) for k in range(L // AB)]
    att = [[_dot_nt(qd[r, head[h]], kd[r, head[h]]) for h in range(nh)] for r in block]
    att = [[jnp.where(mask, a, 0.0).astype(BF16) for a in row] for row in att]
    o_intra = [[_dot(att[k][h], vb[block[k], head[h]]) for h in range(nh)]
               for k in range(len(block))]
    o_inter = [[_dot_nt(qd[chunk[c], head[h]], st_in[c][h]) for h in range(nh)] for c in range(nc)]
    outs = []
    for h in range(nh):
        o = (jnp.concatenate([row[h] for row in o_intra], axis=0)
             + jnp.concatenate([row[h] for row in o_inter], axis=0))
        outs.append(o * lax.rsqrt(jnp.mean(o * o, axis=-1, keepdims=True) + EPS))
    o = jnp.concatenate(outs, axis=1) * ng_ref[...]
    o_ref[0] = (o * (g * _sigmoid(g))).astype(o_ref.dtype)


def _hgrn(proj3, lb, norm_g):
    bsz, seq, _ = proj3.shape
    nh, dh, L = HGRN_HEADS, HGRN_HEAD_DIM, HGRN_TILE
    w = nh * dh

    def col(k):
        return pl.BlockSpec((1, L, w), lambda b, s, k=k: (b, s, k))

    return pl.pallas_call(
        _hgrn_kernel,
        grid=(bsz, seq // L),
        in_specs=[col(0), col(1), col(2), col(3), _const_spec((1, w)), _const_spec((1, w))],
        out_specs=pl.BlockSpec((1, L, w), lambda b, s: (b, s, 0)),
        out_shape=jax.ShapeDtypeStruct((bsz, seq, w), BF16),
        scratch_shapes=[pltpu.VMEM((nh, dh, dh), F32)],
        compiler_params=_params(2),
        name="hgrn2",
    )(proj3, proj3, proj3, proj3, lb.reshape(1, w), norm_g.reshape(1, w))


S5_LC = 16
S5_RC = 16
LANE_BLOCKS = LANES // S5_GROUP


def _s5_tables(a_re, a_im, log_dt, b_re, b_im, c_re, c_im, d_skip):
    G, P = a_re.shape
    Hc = b_re.shape[-1]
    Lc = S5_LC
    dt = jnp.exp(log_dt)[:, None]
    lam, th = a_re * dt, a_im * dt
    mag = jnp.exp(lam)
    abar_re, abar_im = mag * jnp.cos(th), mag * jnp.sin(th)
    den = a_re * a_re + a_im * a_im
    xr, xi = abar_re - 1.0, abar_im
    coef_re = ((xr * a_re + xi * a_im) / den)[..., None]
    coef_im = ((xi * a_re - xr * a_im) / den)[..., None]
    bb_re = coef_re * b_re - coef_im * b_im
    bb_im = coef_re * b_im + coef_im * b_re
    n = jnp.arange(Lc + 1, dtype=F32)[:, None, None]
    pw_re = jnp.exp(n * lam) * jnp.cos(n * th)
    pw_im = jnp.exp(n * lam) * jnp.sin(n * th)
    bt_re, bt_im = bb_re.transpose(0, 2, 1), bb_im.transpose(0, 2, 1)
    cp_re = c_re[None] * pw_re[:Lc, :, None, :] - c_im[None] * pw_im[:Lc, :, None, :]
    cp_im = c_re[None] * pw_im[:Lc, :, None, :] + c_im[None] * pw_re[:Lc, :, None, :]
    kern = jnp.sum(cp_re[:, :, :, None, :] * bt_re[None, :, None, :, :]
                   - cp_im[:, :, :, None, :] * bt_im[None, :, None, :, :], axis=-1)
    skip = d_skip.reshape(G, Hc)[:, :, None] * jnp.eye(Hc, dtype=F32)
    kern = kern.at[0].add(skip)
    kcat = kern.transpose(1, 3, 0, 2).reshape(G, Hc, Lc * Hc)
    pr = pw_re[:Lc][::-1].transpose(1, 0, 2)[:, :, None, :]
    pi = pw_im[:Lc][::-1].transpose(1, 0, 2)[:, :, None, :]
    bs_re = pr * bt_re[:, None] - pi * bt_im[:, None]
    bs_im = pr * bt_im[:, None] + pi * bt_re[:, None]
    bs = jnp.concatenate([bs_re, bs_im], axis=-1).reshape(G, Lc * Hc, 2 * P)
    bs_twin = jnp.concatenate([bs_im, bs_re], axis=-1).reshape(G, Lc * Hc, 2 * P)
    ct_re = c_re.transpose(0, 2, 1)[:, :, None, :]
    ct_im = c_im.transpose(0, 2, 1)[:, :, None, :]
    qr = pw_re[1:].transpose(1, 2, 0)[..., None]
    qi = pw_im[1:].transpose(1, 2, 0)[..., None]
    cs = jnp.concatenate([ct_re * qr - ct_im * qi, -(ct_re * qi + ct_im * qr)],
                         axis=1).reshape(G, 2 * P, Lc * Hc)
    bsw = jnp.concatenate([bs, bs_twin], axis=2).astype(BF16)
    a_n_re, a_n_im = pw_re[Lc], pw_im[Lc]
    m1 = jnp.concatenate([a_n_re, a_n_re], axis=1).reshape(G, 1, 2 * P)
    m2 = jnp.concatenate([-a_n_im, a_n_im], axis=1).reshape(G, 1, 2 * P)
    return kcat, bsw, cs.astype(BF16), m1, m2


def _gelu_tanh(x):
    return 0.5 * x * (1.0 + jnp.tanh(math.sqrt(2.0 / math.pi) * (x + 0.044715 * (x * x * x))))


def _transpose_lane_blocks(vs, blk):
    vs = list(vs)
    d = LANE_BLOCKS // 2
    while d >= 1:
        low = (blk & d) == 0
        for i in range(LANE_BLOCKS):
            if i & d == 0:
                a, b = vs[i], vs[i + d]
                vs[i] = jnp.where(low, a, pltpu.roll(b, d * S5_GROUP, axis=1))
                vs[i + d] = jnp.where(low, pltpu.roll(a, LANES - d * S5_GROUP, axis=1), b)
        d //= 2
    return vs


def _s5_kernel(*refs):
    nq = len(refs) - 15
    u_refs = refs[:nq]
    (kcat_ref, bsw_ref, cs_ref, m1_ref, m2_ref, wglu_ref, bglu_ref, o_ref,
     h_ref, toep_ref, x_ref, y_ref, dh_ref, hs_ref, yt_ref) = refs[nq:]
    nb, tok, _ = u_refs[0].shape
    Lc = S5_LC
    RC = tok // Lc
    R = nb * RC
    G = bsw_ref.shape[0]
    gw = Lc * S5_GROUP
    nhalf = Lc // LANE_BLOCKS

    @pl.when(pl.program_id(0) == 0)
    def _():
        h_ref[...] = jnp.zeros_like(h_ref)
        lane = lax.broadcasted_iota(jnp.int32, (S5_GROUP, gw), 1)

        def build(g, carry):
            kc = kcat_ref[g]
            for s in range(Lc):
                rows = kc if s == 0 else jnp.where(
                    lane >= s * S5_GROUP, pltpu.roll(kc, s * S5_GROUP, axis=1), 0.0)
                toep_ref[g, s * S5_GROUP:(s + 1) * S5_GROUP, :] = rows.astype(BF16)
            return carry

        lax.fori_loop(0, G, build, 0)

    blk = lax.broadcasted_iota(jnp.int32, (R, LANES), 1) // S5_GROUP

    for half in range(nhalf):
        for vq in range(nq):
            ws = _transpose_lane_blocks(
                [u_refs[vq][:, pl.ds(half * LANE_BLOCKS + tl, RC, stride=Lc), :].reshape(R, LANES)
                 for tl in range(LANE_BLOCKS)], blk)
            for gl in range(LANE_BLOCKS):
                col = (nhalf * (LANE_BLOCKS * vq + gl) + half) * LANES
                x_ref[:, col:col + LANES] = ws[gl].astype(BF16)

    sw = 2 * S5_STATE
    for g in range(G):
        xg = x_ref[:, g * gw:(g + 1) * gw]
        y_ref[:, g * gw:(g + 1) * gw] = _dot(xg, toep_ref[g])
        r = _dot(xg, bsw_ref[g])
        dh_ref[0, g] = r[:, :sw]
        dh_ref[1, g] = r[:, sw:]

    hp = [h_ref[0, g] for g in range(G)]
    hq = [h_ref[1, g] for g in range(G)]
    for c in range(RC):
        rows = pl.ds(c, nb, stride=RC)
        for g in range(G):
            hs_ref[g, rows, :] = hp[g]
            m1, m2 = m1_ref[g], m2_ref[g]
            hp[g], hq[g] = (hp[g] * m1 + hq[g] * m2 + dh_ref[0, g, rows, :],
                            hq[g] * m1 - hp[g] * m2 + dh_ref[1, g, rows, :])
    for g in range(G):
        h_ref[0, g] = hp[g]
        h_ref[1, g] = hq[g]

    for g in range(G):
        y_ref[:, g * gw:(g + 1) * gw] += _dot(hs_ref[g].astype(BF16), cs_ref[g])

    for half in range(nhalf):
        for vq in range(nq):
            cols = [(nhalf * (LANE_BLOCKS * vq + gl) + half) * LANES for gl in range(LANE_BLOCKS)]
            ws = _transpose_lane_blocks([y_ref[:, c0:c0 + LANES] for c0 in cols], blk)
            for tl in range(LANE_BLOCKS):
                yt_ref[vq, :, pl.ds(half * LANE_BLOCKS + tl, RC, stride=Lc), :] = (
                    ws[tl].reshape(nb, RC, LANES))

    for b in range(nb):
        z = _gelu_tanh(jnp.concatenate([yt_ref[vq, b] for vq in range(nq)], axis=1))
        gate = _sigmoid(_dot(z.astype(BF16), wglu_ref[...]) + bglu_ref[...])
        o_ref[b] = (z * gate).astype(o_ref.dtype)


def _s5(proj3, u_col_block, tables, w_glu, b_glu):
    bsz, seq, _ = proj3.shape
    kcat, bsw, cs, m1, m2 = tables
    dim = w_glu.shape[0]
    tok = S5_RC * S5_LC
    rows = bsz * S5_RC
    G, _, sw = m1.shape
    gw = kcat.shape[2]
    nq = dim // LANES
    q0 = u_col_block * nq
    u_specs = [pl.BlockSpec((bsz, tok, LANES), lambda i, q=q: (0, i, q0 + q)) for q in range(nq)]
    return pl.pallas_call(
        _s5_kernel,
        grid=(seq // tok,),
        in_specs=u_specs + [_const_spec(kcat.shape), _const_spec(bsw.shape), _const_spec(cs.shape),
                            _const_spec(m1.shape), _const_spec(m2.shape),
                            _const_spec(w_glu.shape), _const_spec((1, dim))],
        out_specs=pl.BlockSpec((bsz, tok, dim), lambda i: (0, i, 0)),
        out_shape=jax.ShapeDtypeStruct((bsz, seq, dim), BF16),
        scratch_shapes=[pltpu.VMEM((2, G, bsz, sw), F32),
                        pltpu.VMEM((G, gw, gw), BF16),
                        pltpu.VMEM((rows, S5_LC * dim), BF16),
                        pltpu.VMEM((rows, S5_LC * dim), F32),
                        pltpu.VMEM((2, G, rows, sw), F32),
                        pltpu.VMEM((G, rows, sw), F32),
                        pltpu.VMEM((nq, bsz, tok, LANES), F32)],
        compiler_params=_params(1),
        name="s5",
    )(*([proj3] * nq), kcat, bsw, cs, m1, m2, w_glu.astype(BF16), b_glu.reshape(1, dim))


FFN_TILE = 1024
FFN_CHUNK = 256
SUBLANES = 8


def _ffn_kernel(tiles_per_seq, final_norm, n_mix, *refs):
    mix = refs[:2 * n_mix]
    (h_ref, g_ref, wa_ref, wu_ref, cw_ref, cb_ref, wo_ref, fg_ref, o_ref,
     prev_ref, abuf_ref, act_ref) = refs[2 * n_mix:]
    tm, fc, hal = FFN_TILE, FFN_CHUNK, SUBLANES
    dff = wa_ref.shape[1]

    @pl.when(pl.program_id(0) % tiles_per_seq == 0)
    def _():
        prev_ref[...] = jnp.zeros_like(prev_ref)

    x = h_ref[...]
    for k in range(n_mix):
        x = x + _dot(mix[2 * k][...], mix[2 * k + 1][...])
    hn = _rms(x, g_ref[...]).astype(BF16)
    for c in range(dff // fc):
        cols = slice(c * fc, (c + 1) * fc)
        a = _dot(hn, wa_ref[:, cols])
        u = _dot(hn, wu_ref[:, cols])
        ab = abuf_ref.at[c % 2]
        ab[0:hal, :] = prev_ref[:, cols]
        ab[hal:hal + tm, :] = a
        prev_ref[:, cols] = a[tm - hal:tm, :]
        a1 = ab[hal - 1:hal - 1 + tm, :]
        a2 = ab[hal - 2:hal - 2 + tm, :]
        w = cw_ref[:, cols]
        conv = a * w[2:3, :] + a1 * w[1:2, :] + a2 * w[0:1, :] + cb_ref[:, cols]
        act_ref[:, cols] = (conv * _sigmoid(conv) * u).astype(BF16)
    y = x + _dot(act_ref[...], wo_ref[...])
    if final_norm:
        y = _rms(y, fg_ref[...])
    o_ref[...] = y


def _ffn(h, seq, layer, g, w_in, conv_w, conv_b, w_out, final_g=None, mix=()):
    t, d = h.shape
    dff = w_out.shape[1]
    tm, fc, hal = FFN_TILE, FFN_CHUNK, SUBLANES
    final_norm = final_g is not None
    fg = final_g if final_norm else g
    mix_specs, mix_args = [], []
    for y, w in mix:
        mix_specs += [pl.BlockSpec((tm, y.shape[1]), lambda i: (i, 0)), _const_spec(w.shape)]
        mix_args += [y, w.astype(BF16)]
    return pl.pallas_call(
        functools.partial(_ffn_kernel, seq // tm, final_norm, len(mix)),
        grid=(t // tm,),
        in_specs=mix_specs + [
                  pl.BlockSpec((tm, d), lambda i: (i, 0)),
                  _const_spec((1, d)),
                  pl.BlockSpec((None, d, dff), lambda i: (layer, 0, 0), pipeline_mode=pl.Buffered(1)),
                  pl.BlockSpec((None, d, dff), lambda i: (layer, 0, 1), pipeline_mode=pl.Buffered(1)),
                  _const_spec((conv_w.shape[0], dff)), _const_spec((1, dff)),
                  pl.BlockSpec((None, dff, d), lambda i: (layer, 0, 0), pipeline_mode=pl.Buffered(1)),
                  _const_spec((1, d))],
        out_specs=pl.BlockSpec((tm, d), lambda i: (i, 0)),
        out_shape=jax.ShapeDtypeStruct((t, d), F32),
        scratch_shapes=[pltpu.VMEM((hal, dff), F32),
                        pltpu.VMEM((2, hal + tm, fc), F32),
                        pltpu.VMEM((tm, dff), BF16)],
        compiler_params=_params(1),
        name="ffn",
    )(*mix_args, h, g.reshape(1, d), w_in, w_in, conv_w, conv_b.reshape(1, dff),
      w_out, fg.reshape(1, d))


MLA_TILE = 512


def _mla_proj_kernel(h_ref, g_ref, win_ref, qg_ref, wuq_ref, kvg_ref, wukn_ref, wuvt_ref,
                     rope_ref, q_ref, kn_ref, kr_ref, vt_ref):
    nh = MLA_HEADS
    hn = _rms(h_ref[...], g_ref[...]).astype(BF16)
    proj = _dot(hn, win_ref[...])
    cq = _rms(proj[:, :MLA_Q_RANK], qg_ref[...]).astype(BF16)
    ckv = _rms(proj[:, MLA_Q_RANK:MLA_Q_RANK + MLA_KV_RANK], kvg_ref[...]).astype(BF16)
    tab = rope_ref[...]

    def rope(x):
        y = x * tab
        return y + pltpu.roll(y, MLA_ROPE, axis=1)

    lane = lax.broadcasted_iota(jnp.int32, tab.shape, 1)
    kr = jnp.where(lane < MLA_ROPE, rope(proj[:, MLA_Q_RANK + MLA_KV_RANK:]), 0.0)
    kr_ref[...] = kr.astype(kr_ref.dtype)
    qf = _dot(cq, wuq_ref[...])
    pieces = []
    for h in range(nh):
        lo = h * MLA_QPAD
        pieces.append(qf[:, lo:lo + LANES])
        pieces.append(rope(qf[:, lo + LANES:lo + MLA_QPAD]))
    q_ref[...] = jnp.concatenate(pieces, axis=1).astype(q_ref.dtype)
    kn_ref[...] = _dot(ckv, wukn_ref[...]).astype(kn_ref.dtype)
    vt = _dot_nt(wuvt_ref[...], ckv)
    row = lax.broadcasted_iota(jnp.int32, vt.shape, 0)
    vt_ref[0] = jnp.where(row % MLA_VROWS >= MLA_V, 1.0, vt).astype(vt_ref.dtype)


def _mla_proj(h, seq, g, w_in, q_norm_g, w_uq, kv_norm_g, w_ukn, w_uvt, rope_tab):
    t, d = h.shape
    tm, nh = MLA_TILE, MLA_HEADS
    tps = seq // tm
    row = lambda n: pl.BlockSpec((tm, n), lambda i: (i, 0))
    return pl.pallas_call(
        _mla_proj_kernel,
        grid=(t // tm,),
        in_specs=[row(d), _const_spec((1, d)), _const_spec(w_in.shape),
                  _const_spec((1, MLA_Q_RANK)), _const_spec(w_uq.shape),
                  _const_spec((1, MLA_KV_RANK)), _const_spec(w_ukn.shape), _const_spec(w_uvt.shape),
                  row(LANES)],
        out_specs=[row(nh * MLA_QPAD), row(nh * MLA_NOPE), row(LANES),
                   pl.BlockSpec((1, nh * MLA_VROWS, tm), lambda i: (i // tps, 0, i % tps))],
        out_shape=[jax.ShapeDtypeStruct((t, nh * MLA_QPAD), BF16),
                   jax.ShapeDtypeStruct((t, nh * MLA_NOPE), BF16),
                   jax.ShapeDtypeStruct((t, LANES), BF16),
                   jax.ShapeDtypeStruct((t // seq, nh * MLA_VROWS, seq), BF16)],
        compiler_params=_params(1),
        name="mla_proj",
    )(h, g.reshape(1, d), w_in, q_norm_g.reshape(1, -1), w_uq, kv_norm_g.reshape(1, -1),
      w_ukn, w_uvt, rope_tab)


ATT_TQ = 512
ATT_TK = 512
ATT_LOOKAHEAD = 0
ATT_DIAG_LOOKAHEAD = 2


def _attn_kernel(q_ref, kn_ref, kr_ref, vt_ref, wo_ref, h_ref, o_ref,
                 ot_ref, m_ref, acc_ref, st0_ref):
    tq, tk = ATT_TQ, ATT_TK
    half = tk // 2
    qi = pl.program_id(1)
    nh = MLA_HEADS
    m_ref[...] = jnp.full(m_ref.shape, -jnp.inf, F32)
    acc_ref[...] = jnp.zeros(acc_ref.shape, F32)

    def scores(off, nk, h, qs=slice(None)):
        rows = pl.ds(off, nk)
        kj = jnp.concatenate([kn_ref[0, rows, h * MLA_NOPE:(h + 1) * MLA_NOPE], kr_ref[0, rows, :]],
                             axis=1)
        return _dot_nt(kj, q_ref[0, qs, h * MLA_QPAD:(h + 1) * MLA_QPAD])

    def shifted_logits(h, st, qs=slice(None)):
        m = m_ref[h, :, qs]
        m_new = jnp.maximum(m, jnp.max(st, axis=0, keepdims=True))
        m_ref[h, :, qs] = m_new
        return jnp.exp2(m - m_new), (st - m_new).astype(BF16)

    def accumulate(off, nk, h, alpha, d, qs=slice(None)):
        acc_ref[h, :, qs] = alpha * acc_ref[h, :, qs] + _dot(
            vt_ref[0, h * MLA_VROWS:(h + 1) * MLA_VROWS, pl.ds(off, nk)], jnp.exp2(d))

    def key_tile(off, off_next):
        pending = {0: st0_ref[...]}
        todo = list(range(1, nh)) + [None]

        def issue():
            if todo:
                h = todo.pop(0)
                if h is None:
                    st0_ref[...] = scores(off_next, tk, 0)
                else:
                    pending[h] = scores(off, tk, h)

        for _ in range(ATT_LOOKAHEAD):
            issue()
        for h in range(nh):
            alpha, d = shifted_logits(h, pending.pop(h))
            issue()
            accumulate(off, tk, h, alpha, d)

    def diag_tile(off):
        causal = (lax.broadcasted_iota(jnp.int32, (half, tq), 0)
                  <= lax.broadcasted_iota(jnp.int32, (half, tq), 1))
        upper = slice(half, tq)
        units = []
        for h in range(nh):
            units.append((h, off, slice(None), causal))
            units.append((h, off + half, upper, causal[:, :half]))
        pending = {0: st0_ref[:half, :], 1: st0_ref[half:, half:]}
        todo = list(range(2, len(units)))

        def issue():
            if todo:
                u = todo.pop(0)
                h, koff, qs, _ = units[u]
                pending[u] = scores(koff, half, h, qs)

        for _ in range(ATT_DIAG_LOOKAHEAD):
            issue()
        for u, (h, koff, qs, mask) in enumerate(units):
            alpha, d = shifted_logits(h, jnp.where(mask, pending.pop(u), -jnp.inf), qs)
            issue()
            accumulate(koff, half, h, alpha, d, qs)

    st0_ref[...] = scores(0, tk, 0)

    def body(j, carry):
        key_tile(pl.multiple_of(j * tk, tk), pl.multiple_of((j + 1) * tk, tk))
        return carry

    lax.fori_loop(0, qi, body, 0)
    diag_tile(pl.multiple_of(qi * tk, tk))
    for h in range(nh):
        acc = acc_ref[h]
        ot_ref[h * MLA_V:(h + 1) * MLA_V, :] = (acc[:MLA_V] / acc[MLA_V:MLA_V + 1]).astype(BF16)
    o_ref[0] = h_ref[0] + _dot_tn(ot_ref[...], wo_ref[...])


def _mla_attn(q, kn, kr, vt, w_out, h3):
    bsz, seq, d = h3.shape
    tq = ATT_TQ
    nh = MLA_HEADS
    full = lambda n: pl.BlockSpec((1, seq, n), lambda b, i: (b, 0, 0), pipeline_mode=pl.Buffered(1))
    return pl.pallas_call(
        _attn_kernel,
        grid=(bsz, seq // tq),
        in_specs=[pl.BlockSpec((1, tq, nh * MLA_QPAD), lambda b, i: (b, i, 0)),
                  full(nh * MLA_NOPE), full(LANES),
                  pl.BlockSpec((1, nh * MLA_VROWS, seq), lambda b, i: (b, 0, 0),
                               pipeline_mode=pl.Buffered(1)),
                  _const_spec(w_out.shape),
                  pl.BlockSpec((1, tq, d), lambda b, i: (b, i, 0))],
        out_specs=pl.BlockSpec((1, tq, d), lambda b, i: (b, i, 0)),
        out_shape=jax.ShapeDtypeStruct((bsz, seq, d), F32),
        scratch_shapes=[pltpu.VMEM((nh * MLA_V, tq), BF16),
                        pltpu.VMEM((nh, 1, tq), F32),
                        pltpu.VMEM((nh, MLA_VROWS, tq), F32),
                        pltpu.VMEM((ATT_TK, tq), F32)],
        compiler_params=_params(2),
        name="mla_attn",
    )(q, kn, kr, vt, w_out, h3)


def _mla_weights(w_in, w_uq, w_ukv):
    nh, r = MLA_HEADS, MLA_ROPE
    qk = MLA_NOPE + MLA_ROPE

    def swap(w):
        return jnp.concatenate([-w[..., r // 2:], w[..., :r // 2]], axis=-1)

    k_rope = w_in[:, MLA_Q_RANK + MLA_KV_RANK:]
    w_in2 = jnp.concatenate([w_in, swap(k_rope)], axis=1)
    wq = w_uq.reshape(MLA_Q_RANK, nh, qk) * (qk ** -0.5 * math.log2(math.e))
    wq2 = jnp.concatenate([wq, swap(wq[..., MLA_NOPE:])], axis=-1).reshape(MLA_Q_RANK, nh * MLA_QPAD)
    wkv = w_ukv.reshape(MLA_KV_RANK, nh, 2, MLA_NOPE)
    w_ukn = wkv[:, :, 0, :].reshape(MLA_KV_RANK, nh * MLA_NOPE)
    w_uvt = jnp.pad(wkv[:, :, 1, :], ((0, 0), (0, 0), (0, MLA_VROWS - MLA_V)))
    w_uvt = w_uvt.reshape(MLA_KV_RANK, nh * MLA_VROWS).T
    return w_in2.astype(BF16), wq2.astype(BF16), w_ukn.astype(BF16), w_uvt.astype(BF16)


def _rope_table(positions):
    r = MLA_ROPE
    freqs = ROPE_THETA ** (-jnp.arange(0, r, 2, dtype=F32) / r)
    freq4 = jnp.tile(freqs, LANES // (r // 2))
    phase = jnp.where(jnp.arange(LANES) < r, 0.0, math.pi / 2).astype(F32)
    return jnp.cos(positions.astype(F32).reshape(-1, 1) * freq4 - phase)


def kernel(x, positions, norm_mix_g, norm_ffn_g, final_norm_g, even_w_in, hgrn_lb_logits, hgrn_norm_g, s5_a_re, s5_a_im, s5_log_dt, s5_b_re, s5_b_im, s5_c_re, s5_c_im, s5_d, s5_w_glu, s5_b_glu, even_w_out, odd_w_in, mla_q_norm_g, mla_w_uq, mla_kv_norm_g, mla_w_ukv, odd_w_out, ffn_w_in, ffn_conv_w, ffn_conv_b, ffn_w_out):
    bsz, seq, d = x.shape
    t = bsz * seq
    depth = norm_mix_g.shape[0]
    lower_bounds = jnp.cumsum(jax.nn.softmax(hgrn_lb_logits.astype(F32), axis=0), axis=0)
    hgrn_dim = HGRN_HEADS * HGRN_HEAD_DIM
    h = x.reshape(t, d)
    ffn_w_in_bf, ffn_w_out_bf = ffn_w_in.astype(BF16), ffn_w_out.astype(BF16)
    for layer in range(depth):
        j = layer // 2
        mix = ()
        if layer % 2 == 0:
            proj = _norm_proj(h, norm_mix_g[layer], even_w_in[j].astype(BF16))
            proj3 = proj.reshape(bsz, seq, -1)
            ya = _hgrn(proj3, lower_bounds[j], hgrn_norm_g[j])
            s5_dim = s5_d.shape[1]
            tables = _s5_tables(s5_a_re[j], s5_a_im[j], s5_log_dt[j], s5_b_re[j], s5_b_im[j],
                                s5_c_re[j], s5_c_im[j], s5_d[j])
            yb = _s5(proj3, (4 * hgrn_dim) // s5_dim, tables, s5_w_glu[j], s5_b_glu[j])
            mix = ((ya.reshape(t, -1), even_w_out[j][:hgrn_dim]),
                   (yb.reshape(t, -1), even_w_out[j][hgrn_dim:]))
        else:
            w_in2, wq2, w_ukn, w_uvt = _mla_weights(odd_w_in[j], mla_w_uq[j], mla_w_ukv[j])
            q, kn, kr, vt = _mla_proj(h, seq, norm_mix_g[layer], w_in2, mla_q_norm_g[j], wq2,
                                      mla_kv_norm_g[j], w_ukn, w_uvt, _rope_table(positions))
            r3 = lambda a: a.reshape(bsz, seq, -1)
            h = _mla_attn(r3(q), r3(kn), r3(kr), vt, odd_w_out[j].astype(BF16),
                          h.reshape(bsz, seq, d)).reshape(t, d)
        last = layer == depth - 1
        h = _ffn(h, seq, layer, norm_ffn_g[layer], ffn_w_in_bf, ffn_conv_w[layer], ffn_conv_b[layer],
                 ffn_w_out_bf, final_norm_g if last else None, mix)
    return h.reshape(bsz, seq, d)
```

```python
import functools
import math

import jax
import jax.numpy as jnp
from jax import lax
from jax.experimental import pallas as pl
from jax.experimental.pallas import tpu as pltpu

F32 = jnp.float32
BF16 = jnp.bfloat16

EPS = 1e-6
HGRN_HEADS = 4
HGRN_HEAD_DIM = 128
HGRN_CHUNK = 64
S5_GROUP = 16
S5_STATE = 64
MLA_HEADS = 8
MLA_Q_RANK = 384
MLA_KV_RANK = 256
MLA_NOPE = 128
MLA_ROPE = 64
MLA_V = 128
ROPE_THETA = 10000.0
LANES = 128
MLA_QPAD = 2 * LANES
BF16_SUBLANES = 16
MLA_VROWS = MLA_V + BF16_SUBLANES

VMEM_LIMIT = 56 * 1024 * 1024


def _params(n_axes, vmem=VMEM_LIMIT):
    return pltpu.CompilerParams(dimension_semantics=("arbitrary",) * n_axes,
                                vmem_limit_bytes=vmem)


def _const_spec(shape):
    zeros = (0,) * len(shape)
    return pl.BlockSpec(shape, lambda *_: zeros, pipeline_mode=pl.Buffered(1))


def _rms(x, g):
    return x * lax.rsqrt(jnp.mean(x * x, axis=-1, keepdims=True) + EPS) * g


def _sigmoid(x):
    return 0.5 * jnp.tanh(0.5 * x) + 0.5


def _dot(a, b):
    return jnp.dot(a, b, preferred_element_type=F32)


def _dot_nt(a, b):
    return lax.dot_general(a, b, (((1,), (1,)), ((), ())), preferred_element_type=F32)


def _dot_tn(a, b):
    return lax.dot_general(a, b, (((0,), (0,)), ((), ())), preferred_element_type=F32)


def _norm_proj_kernel(h_ref, g_ref, w_ref, o_ref):
    hn = _rms(h_ref[...], g_ref[...]).astype(BF16)
    o_ref[...] = _dot(hn, w_ref[...]).astype(o_ref.dtype)


def _norm_proj(h, g, w, tm=1024):
    t, d = h.shape
    n = w.shape[1]
    return pl.pallas_call(
        _norm_proj_kernel,
        grid=(t // tm,),
        in_specs=[pl.BlockSpec((tm, d), lambda i: (i, 0)),
                  _const_spec((1, d)),
                  _const_spec((d, n))],
        out_specs=pl.BlockSpec((tm, n), lambda i: (i, 0)),
        out_shape=jax.ShapeDtypeStruct((t, n), F32),
        compiler_params=_params(1),
        name="even_in",
    )(h, g.reshape(1, d), w)


HGRN_TILE = 512
HGRN_ATT_BLOCK = 256


def _hgrn_kernel(q_ref, f_ref, i_ref, g_ref, lb_ref, ng_ref, o_ref, st_ref):
    L, C, AB = HGRN_TILE, HGRN_CHUNK, HGRN_ATT_BLOCK
    dh = HGRN_HEAD_DIM

    nh = HGRN_HEADS

    @pl.when(pl.program_id(1) == 0)
    def _():
        st_ref[...] = jnp.zeros_like(st_ref)

    q = q_ref[0]
    f = f_ref[0]
    g = g_ref[0]
    lb = lb_ref[...]
    forget = lb + (1.0 - lb) * _sigmoid(f)
    kh = 1.0 - forget
    ri = lax.broadcasted_iota(jnp.int32, (AB, AB), 0)
    ci = lax.broadcasted_iota(jnp.int32, (AB, AB), 1)
    mask = jnp.logical_and((ri // C) == (ci // C), ci <= ri)

    logf = jnp.log(forget)
    lf_hi = logf.astype(BF16)
    lf_lo = (logf - lf_hi.astype(F32)).astype(BF16)
    tri = jnp.where(mask, 1.0, 0.0).astype(BF16)
    b = jnp.concatenate(
        [_dot(tri, lf_hi[r0:r0 + AB]) + _dot(tri, lf_lo[r0:r0 + AB]) for r0 in range(0, L, AB)],
        axis=0)
    qd = (q * jnp.exp(b)).astype(BF16)
    kd = (kh * jnp.exp(-b)).astype(BF16)
    vb = i_ref[0].astype(BF16)
    nc = L // C
    chunk = [slice(c * C, (c + 1) * C) for c in range(nc)]
    head = [slice(h * dh, (h + 1) * dh) for h in range(nh)]
    bl = [b[c * C + C - 1:(c + 1) * C, :] for c in range(nc)]
    kdec = (kh * jnp.exp(jnp.concatenate(
        [jnp.broadcast_to(bl[c], (C, nh * dh)) for c in range(nc)], axis=0) - b)).astype(BF16)
    decay = [jnp.exp(x) for x in bl]
    ds = [[_dot_tn(vb[chunk[c], head[h]], kdec[chunk[c], head[h]]) for h in range(nh)]
          for c in range(nc)]
    st = [st_ref[h] for h in range(nh)]
    st_in = []
    for c in range(nc):
        st_in.append([s.astype(BF16) for s in st])
        st = [st[h] * decay[c][:, head[h]] + ds[c][h] for h in range(nh)]
    for h in range(nh):
        st_ref[h] = st[h]
    block = [slice(k * AB, (k + 1) * AB) for k in range(L // AB)]
    att = [[_dot_nt(qd[r, head[h]], kd[r, head[h]]) for h in range(nh)] for r in block]
    att = [[jnp.where(mask, a, 0.0).astype(BF16) for a in row] for row in att]
    o_intra = [[_dot(att[k][h], vb[block[k], head[h]]) for h in range(nh)]
               for k in range(len(block))]
    o_inter = [[_dot_nt(qd[chunk[c], head[h]], st_in[c][h]) for h in range(nh)] for c in range(nc)]
    outs = []
    for h in range(nh):
        o = (jnp.concatenate([row[h] for row in o_intra], axis=0)
             + jnp.concatenate([row[h] for row in o_inter], axis=0))
        outs.append(o * lax.rsqrt(jnp.mean(o * o, axis=-1, keepdims=True) + EPS))
    o = jnp.concatenate(outs, axis=1) * ng_ref[...]
    o_ref[0] = (o * (g * _sigmoid(g))).astype(o_ref.dtype)


def _hgrn(proj3, lb, norm_g):
    bsz, seq, _ = proj3.shape
    nh, dh, L = HGRN_HEADS, HGRN_HEAD_DIM, HGRN_TILE
    w = nh * dh

    def col(k):
        return pl.BlockSpec((1, L, w), lambda b, s, k=k: (b, s, k))

    return pl.pallas_call(
        _hgrn_kernel,
        grid=(bsz, seq // L),
        in_specs=[col(0), col(1), col(2), col(3), _const_spec((1, w)), _const_spec((1, w))],
        out_specs=pl.BlockSpec((1, L, w), lambda b, s: (b, s, 0)),
        out_shape=jax.ShapeDtypeStruct((bsz, seq, w), BF16),
        scratch_shapes=[pltpu.VMEM((nh, dh, dh), F32)],
        compiler_params=_params(2),
        name="hgrn2",
    )(proj3, proj3, proj3, proj3, lb.reshape(1, w), norm_g.reshape(1, w))


S5_LC = 16
S5_RC = 16
LANE_BLOCKS = LANES // S5_GROUP


def _s5_tables(a_re, a_im, log_dt, b_re, b_im, c_re, c_im, d_skip):
    G, P = a_re.shape
    Hc = b_re.shape[-1]
    Lc = S5_LC
    dt = jnp.exp(log_dt)[:, None]
    lam, th = a_re * dt, a_im * dt
    mag = jnp.exp(lam)
    abar_re, abar_im = mag * jnp.cos(th), mag * jnp.sin(th)
    den = a_re * a_re + a_im * a_im
    xr, xi = abar_re - 1.0, abar_im
    coef_re = ((xr * a_re + xi * a_im) / den)[..., None]
    coef_im = ((xi * a_re - xr * a_im) / den)[..., None]
    bb_re = coef_re * b_re - coef_im * b_im
    bb_im = coef_re * b_im + coef_im * b_re
    n = jnp.arange(Lc + 1, dtype=F32)[:, None, None]
    pw_re = jnp.exp(n * lam) * jnp.cos(n * th)
    pw_im = jnp.exp(n * lam) * jnp.sin(n * th)
    bt_re, bt_im = bb_re.transpose(0, 2, 1), bb_im.transpose(0, 2, 1)
    cp_re = c_re[None] * pw_re[:Lc, :, None, :] - c_im[None] * pw_im[:Lc, :, None, :]
    cp_im = c_re[None] * pw_im[:Lc, :, None, :] + c_im[None] * pw_re[:Lc, :, None, :]
    kern = jnp.sum(cp_re[:, :, :, None, :] * bt_re[None, :, None, :, :]
                   - cp_im[:, :, :, None, :] * bt_im[None, :, None, :, :], axis=-1)
    skip = d_skip.reshape(G, Hc)[:, :, None] * jnp.eye(Hc, dtype=F32)
    kern = kern.at[0].add(skip)
    kcat = kern.transpose(1, 3, 0, 2).reshape(G, Hc, Lc * Hc)
    pr = pw_re[:Lc][::-1].transpose(1, 0, 2)[:, :, None, :]
    pi = pw_im[:Lc][::-1].transpose(1, 0, 2)[:, :, None, :]
    bs_re = pr * bt_re[:, None] - pi * bt_im[:, None]
    bs_im = pr * bt_im[:, None] + pi * bt_re[:, None]
    bs = jnp.concatenate([bs_re, bs_im], axis=-1).reshape(G, Lc * Hc, 2 * P)
    bs_twin = jnp.concatenate([bs_im, bs_re], axis=-1).reshape(G, Lc * Hc, 2 * P)
    ct_re = c_re.transpose(0, 2, 1)[:, :, None, :]
    ct_im = c_im.transpose(0, 2, 1)[:, :, None, :]
    qr = pw_re[1:].transpose(1, 2, 0)[..., None]
    qi = pw_im[1:].transpose(1, 2, 0)[..., None]
    cs = jnp.concatenate([ct_re * qr - ct_im * qi, -(ct_re * qi + ct_im * qr)],
                         axis=1).reshape(G, 2 * P, Lc * Hc)
    bsw = jnp.concatenate([bs, bs_twin], axis=2).astype(BF16)
    a_n_re, a_n_im = pw_re[Lc], pw_im[Lc]
    m1 = jnp.concatenate([a_n_re, a_n_re], axis=1).reshape(G, 1, 2 * P)
    m2 = jnp.concatenate([-a_n_im, a_n_im], axis=1).reshape(G, 1, 2 * P)
    return kcat, bsw, cs.astype(BF16), m1, m2


def _gelu_tanh(x):
    return 0.5 * x * (1.0 + jnp.tanh(math.sqrt(2.0 / math.pi) * (x + 0.044715 * (x * x * x))))


def _transpose_lane_blocks(vs, blk):
    vs = list(vs)
    d = LANE_BLOCKS // 2
    while d >= 1:
        low = (blk & d) == 0
        for i in range(LANE_BLOCKS):
            if i & d == 0:
                a, b = vs[i], vs[i + d]
                vs[i] = jnp.where(low, a, pltpu.roll(b, d * S5_GROUP, axis=1))
                vs[i + d] = jnp.where(low, pltpu.roll(a, LANES - d * S5_GROUP, axis=1), b)
        d //= 2
    return vs


def _s5_kernel(*refs):
    nq = len(refs) - 15
    u_refs = refs[:nq]
    (kcat_ref, bsw_ref, cs_ref, m1_ref, m2_ref, wglu_ref, bglu_ref, o_ref,
     h_ref, toep_ref, x_ref, y_ref, dh_ref, hs_ref, yt_ref) = refs[nq:]
    nb, tok, _ = u_refs[0].shape
    Lc = S5_LC
    RC = tok // Lc
    R = nb * RC
    G = bsw_ref.shape[0]
    gw = Lc * S5_GROUP
    nhalf = Lc // LANE_BLOCKS

    @pl.when(pl.program_id(0) == 0)
    def _():
        h_ref[...] = jnp.zeros_like(h_ref)
        lane = lax.broadcasted_iota(jnp.int32, (S5_GROUP, gw), 1)

        def build(g, carry):
            kc = kcat_ref[g]
            for s in range(Lc):
                rows = kc if s == 0 else jnp.where(
                    lane >= s * S5_GROUP, pltpu.roll(kc, s * S5_GROUP, axis=1), 0.0)
                toep_ref[g, s * S5_GROUP:(s + 1) * S5_GROUP, :] = rows.astype(BF16)
            return carry

        lax.fori_loop(0, G, build, 0)

    blk = lax.broadcasted_iota(jnp.int32, (R, LANES), 1) // S5_GROUP

    for half in range(nhalf):
        for vq in range(nq):
            ws = _transpose_lane_blocks(
                [u_refs[vq][:, pl.ds(half * LANE_BLOCKS + tl, RC, stride=Lc), :].reshape(R, LANES)
                 for tl in range(LANE_BLOCKS)], blk)
            for gl in range(LANE_BLOCKS):
                col = (nhalf * (LANE_BLOCKS * vq + gl) + half) * LANES
                x_ref[:, col:col + LANES] = ws[gl].astype(BF16)

    sw = 2 * S5_STATE
    for g in range(G):
        xg = x_ref[:, g * gw:(g + 1) * gw]
        y_ref[:, g * gw:(g + 1) * gw] = _dot(xg, toep_ref[g])
        r = _dot(xg, bsw_ref[g])
        dh_ref[0, g] = r[:, :sw]
        dh_ref[1, g] = r[:, sw:]

    hp = [h_ref[0, g] for g in range(G)]
    hq = [h_ref[1, g] for g in range(G)]
    for c in range(RC):
        rows = pl.ds(c, nb, stride=RC)
        for g in range(G):
            hs_ref[g, rows, :] = hp[g]
            m1, m2 = m1_ref[g], m2_ref[g]
            hp[g], hq[g] = (hp[g] * m1 + hq[g] * m2 + dh_ref[0, g, rows, :],
                            hq[g] * m1 - hp[g] * m2 + dh_ref[1, g, rows, :])
    for g in range(G):
        h_ref[0, g] = hp[g]
        h_ref[1, g] = hq[g]

    for g in range(G):
        y_ref[:, g * gw:(g + 1) * gw] += _dot(hs_ref[g].astype(BF16), cs_ref[g])

    for half in range(nhalf):
        for vq in range(nq):
            cols = [(nhalf * (LANE_BLOCKS * vq + gl) + half) * LANES for gl in range(LANE_BLOCKS)]
            ws = _transpose_lane_blocks([y_ref[:, c0:c0 + LANES] for c0 in cols], blk)
            for tl in range(LANE_BLOCKS):
                yt_ref[vq, :, pl.ds(half * LANE_BLOCKS + tl, RC, stride=Lc), :] = (
                    ws[tl].reshape(nb, RC, LANES))

    for b in range(nb):
        z = _gelu_tanh(jnp.concatenate([yt_ref[vq, b] for vq in range(nq)], axis=1))
        gate = _sigmoid(_dot(z.astype(BF16), wglu_ref[...]) + bglu_ref[...])
        o_ref[b] = (z * gate).astype(o_ref.dtype)


def _s5(proj3, u_col_block, tables, w_glu, b_glu):
    bsz, seq, _ = proj3.shape
    kcat, bsw, cs, m1, m2 = tables
    dim = w_glu.shape[0]
    tok = S5_RC * S5_LC
    rows = bsz * S5_RC
    G, _, sw = m1.shape
    gw = kcat.shape[2]
    nq = dim // LANES
    q0 = u_col_block * nq
    u_specs = [pl.BlockSpec((bsz, tok, LANES), lambda i, q=q: (0, i, q0 + q)) for q in range(nq)]
    return pl.pallas_call(
        _s5_kernel,
        grid=(seq // tok,),
        in_specs=u_specs + [_const_spec(kcat.shape), _const_spec(bsw.shape), _const_spec(cs.shape),
                            _const_spec(m1.shape), _const_spec(m2.shape),
                            _const_spec(w_glu.shape), _const_spec((1, dim))],
        out_specs=pl.BlockSpec((bsz, tok, dim), lambda i: (0, i, 0)),
        out_shape=jax.ShapeDtypeStruct((bsz, seq, dim), BF16),
        scratch_shapes=[pltpu.VMEM((2, G, bsz, sw), F32),
                        pltpu.VMEM((G, gw, gw), BF16),
                        pltpu.VMEM((rows, S5_LC * dim), BF16),
                        pltpu.VMEM((rows, S5_LC * dim), F32),
                        pltpu.VMEM((2, G, rows, sw), F32),
                        pltpu.VMEM((G, rows, sw), F32),
                        pltpu.VMEM((nq, bsz, tok, LANES), F32)],
        compiler_params=_params(1),
        name="s5",
    )(*([proj3] * nq), kcat, bsw, cs, m1, m2, w_glu.astype(BF16), b_glu.reshape(1, dim))


FFN_TILE = 1024
FFN_CHUNK = 256
SUBLANES = 8


def _ffn_kernel(tiles_per_seq, final_norm, n_mix, *refs):
    mix = refs[:2 * n_mix]
    (h_ref, g_ref, wa_ref, wu_ref, cw_ref, cb_ref, wo_ref, fg_ref, o_ref,
     prev_ref, abuf_ref, act_ref) = refs[2 * n_mix:]
    tm, fc, hal = FFN_TILE, FFN_CHUNK, SUBLANES
    dff = wa_ref.shape[1]

    @pl.when(pl.program_id(0) % tiles_per_seq == 0)
    def _():
        prev_ref[...] = jnp.zeros_like(prev_ref)

    x = h_ref[...]
    for k in range(n_mix):
        x = x + _dot(mix[2 * k][...], mix[2 * k + 1][...])
    hn = _rms(x, g_ref[...]).astype(BF16)
    for c in range(dff // fc):
        cols = slice(c * fc, (c + 1) * fc)
        a = _dot(hn, wa_ref[:, cols])
        u = _dot(hn, wu_ref[:, cols])
        ab = abuf_ref.at[c % 2]
        ab[0:hal, :] = prev_ref[:, cols]
        ab[hal:hal + tm, :] = a
        prev_ref[:, cols] = a[tm - hal:tm, :]
        a1 = ab[hal - 1:hal - 1 + tm, :]
        a2 = ab[hal - 2:hal - 2 + tm, :]
        w = cw_ref[:, cols]
        conv = a * w[2:3, :] + a1 * w[1:2, :] + a2 * w[0:1, :] + cb_ref[:, cols]
        act_ref[:, cols] = (conv * _sigmoid(conv) * u).astype(BF16)
    y = x + _dot(act_ref[...], wo_ref[...])
    if final_norm:
        y = _rms(y, fg_ref[...])
    o_ref[...] = y


def _ffn(h, seq, layer, g, w_in, conv_w, conv_b, w_out, final_g=None, mix=()):
    t, d = h.shape
    dff = w_out.shape[1]
    tm, fc, hal = FFN_TILE, FFN_CHUNK, SUBLANES
    final_norm = final_g is not None
    fg = final_g if final_norm else g
    mix_specs, mix_args = [], []
    for y, w in mix:
        mix_specs += [pl.BlockSpec((tm, y.shape[1]), lambda i: (i, 0)), _const_spec(w.shape)]
        mix_args += [y, w.astype(BF16)]
    return pl.pallas_call(
        functools.partial(_ffn_kernel, seq // tm, final_norm, len(mix)),
        grid=(t // tm,),
        in_specs=mix_specs + [
                  pl.BlockSpec((tm, d), lambda i: (i, 0)),
                  _const_spec((1, d)),
                  pl.BlockSpec((None, d, dff), lambda i: (layer, 0, 0), pipeline_mode=pl.Buffered(1)),
                  pl.BlockSpec((None, d, dff), lambda i: (layer, 0, 1), pipeline_mode=pl.Buffered(1)),
                  _const_spec((conv_w.shape[0], dff)), _const_spec((1, dff)),
                  pl.BlockSpec((None, dff, d), lambda i: (layer, 0, 0), pipeline_mode=pl.Buffered(1)),
                  _const_spec((1, d))],
        out_specs=pl.BlockSpec((tm, d), lambda i: (i, 0)),
        out_shape=jax.ShapeDtypeStruct((t, d), F32),
        scratch_shapes=[pltpu.VMEM((hal, dff), F32),
                        pltpu.VMEM((2, hal + tm, fc), F32),
                        pltpu.VMEM((tm, dff), BF16)],
        compiler_params=_params(1),
        name="ffn",
    )(*mix_args, h, g.reshape(1, d), w_in, w_in, conv_w, conv_b.reshape(1, dff),
      w_out, fg.reshape(1, d))


MLA_TILE = 1024


def _mla_proj_kernel(h_ref, g_ref, win_ref, qg_ref, wuq_ref, kvg_ref, wukn_ref, wuvt_ref,
                     rope_ref, q_ref, kn_ref, kr_ref, vt_ref):
    nh = MLA_HEADS
    hn = _rms(h_ref[...], g_ref[...]).astype(BF16)
    proj = _dot(hn, win_ref[...])
    cq = _rms(proj[:, :MLA_Q_RANK], qg_ref[...]).astype(BF16)
    ckv = _rms(proj[:, MLA_Q_RANK:MLA_Q_RANK + MLA_KV_RANK], kvg_ref[...]).astype(BF16)
    tab = rope_ref[...]

    def rope(x):
        y = x * tab
        return y + pltpu.roll(y, MLA_ROPE, axis=1)

    lane = lax.broadcasted_iota(jnp.int32, tab.shape, 1)
    kr = jnp.where(lane < MLA_ROPE, rope(proj[:, MLA_Q_RANK + MLA_KV_RANK:]), 0.0)
    kr_ref[...] = kr.astype(kr_ref.dtype)
    qf = _dot(cq, wuq_ref[...])
    pieces = []
    for h in range(nh):
        lo = h * MLA_QPAD
        pieces.append(qf[:, lo:lo + LANES])
        pieces.append(rope(qf[:, lo + LANES:lo + MLA_QPAD]))
    q_ref[...] = jnp.concatenate(pieces, axis=1).astype(q_ref.dtype)
    kn_ref[...] = _dot(ckv, wukn_ref[...]).astype(kn_ref.dtype)
    vt = _dot_nt(wuvt_ref[...], ckv)
    row = lax.broadcasted_iota(jnp.int32, vt.shape, 0)
    vt_ref[0] = jnp.where(row % MLA_VROWS >= MLA_V, 1.0, vt).astype(vt_ref.dtype)


def _mla_proj(h, seq, g, w_in, q_norm_g, w_uq, kv_norm_g, w_ukn, w_uvt, rope_tab):
    t, d = h.shape
    tm, nh = MLA_TILE, MLA_HEADS
    tps = seq // tm
    row = lambda n: pl.BlockSpec((tm, n), lambda i: (i, 0))
    return pl.pallas_call(
        _mla_proj_kernel,
        grid=(t // tm,),
        in_specs=[row(d), _const_spec((1, d)), _const_spec(w_in.shape),
                  _const_spec((1, MLA_Q_RANK)), _const_spec(w_uq.shape),
                  _const_spec((1, MLA_KV_RANK)), _const_spec(w_ukn.shape), _const_spec(w_uvt.shape),
                  row(LANES)],
        out_specs=[row(nh * MLA_QPAD), row(nh * MLA_NOPE), row(LANES),
                   pl.BlockSpec((1, nh * MLA_VROWS, tm), lambda i: (i // tps, 0, i % tps))],
        out_shape=[jax.ShapeDtypeStruct((t, nh * MLA_QPAD), BF16),
                   jax.ShapeDtypeStruct((t, nh * MLA_NOPE), BF16),
                   jax.ShapeDtypeStruct((t, LANES), BF16),
                   jax.ShapeDtypeStruct((t // seq, nh * MLA_VROWS, seq), BF16)],
        compiler_params=_params(1),
        name="mla_proj",
    )(h, g.reshape(1, d), w_in, q_norm_g.reshape(1, -1), w_uq, kv_norm_g.reshape(1, -1),
      w_ukn, w_uvt, rope_tab)


ATT_TQ = 512
ATT_TK = 512
ATT_LOOKAHEAD = 0
ATT_DIAG_LOOKAHEAD = 2


def _attn_kernel(q_ref, kn_ref, kr_ref, vt_ref, wo_ref, h_ref, o_ref,
                 ot_ref, m_ref, acc_ref, st0_ref):
    tq, tk = ATT_TQ, ATT_TK
    half = tk // 2
    qi = pl.program_id(1)
    nh = MLA_HEADS
    m_ref[...] = jnp.full(m_ref.shape, -jnp.inf, F32)
    acc_ref[...] = jnp.zeros(acc_ref.shape, F32)

    def scores(off, nk, h, qs=slice(None)):
        rows = pl.ds(off, nk)
        kj = jnp.concatenate([kn_ref[0, rows, h * MLA_NOPE:(h + 1) * MLA_NOPE], kr_ref[0, rows, :]],
                             axis=1)
        return _dot_nt(kj, q_ref[0, qs, h * MLA_QPAD:(h + 1) * MLA_QPAD])

    def shifted_logits(h, st, qs=slice(None)):
        m = m_ref[h, :, qs]
        m_new = jnp.maximum(m, jnp.max(st, axis=0, keepdims=True))
        m_ref[h, :, qs] = m_new
        return jnp.exp2(m - m_new), (st - m_new).astype(BF16)

    def accumulate(off, nk, h, alpha, d, qs=slice(None)):
        acc_ref[h, :, qs] = alpha * acc_ref[h, :, qs] + _dot(
            vt_ref[0, h * MLA_VROWS:(h + 1) * MLA_VROWS, pl.ds(off, nk)], jnp.exp2(d))

    def key_tile(off, off_next):
        pending = {0: st0_ref[...]}
        todo = list(range(1, nh)) + [None]

        def issue():
            if todo:
                h = todo.pop(0)
                if h is None:
                    st0_ref[...] = scores(off_next, tk, 0)
                else:
                    pending[h] = scores(off, tk, h)

        for _ in range(ATT_LOOKAHEAD):
            issue()
        for h in range(nh):
            alpha, d = shifted_logits(h, pending.pop(h))
            issue()
            accumulate(off, tk, h, alpha, d)

    def diag_tile(off):
        causal = (lax.broadcasted_iota(jnp.int32, (half, tq), 0)
                  <= lax.broadcasted_iota(jnp.int32, (half, tq), 1))
        upper = slice(half, tq)
        units = []
        for h in range(nh):
            units.append((h, off, slice(None), causal))
            units.append((h, off + half, upper, causal[:, :half]))
        pending = {0: st0_ref[:half, :], 1: st0_ref[half:, half:]}
        todo = list(range(2, len(units)))

        def issue():
            if todo:
                u = todo.pop(0)
                h, koff, qs, _ = units[u]
                pending[u] = scores(koff, half, h, qs)

        for _ in range(ATT_DIAG_LOOKAHEAD):
            issue()
        for u, (h, koff, qs, mask) in enumerate(units):
            alpha, d = shifted_logits(h, jnp.where(mask, pending.pop(u), -jnp.inf), qs)
            issue()
            accumulate(koff, half, h, alpha, d, qs)

    st0_ref[...] = scores(0, tk, 0)

    def body(j, carry):
        key_tile(pl.multiple_of(j * tk, tk), pl.multiple_of((j + 1) * tk, tk))
        return carry

    lax.fori_loop(0, qi, body, 0)
    diag_tile(pl.multiple_of(qi * tk, tk))
    for h in range(nh):
        acc = acc_ref[h]
        ot_ref[h * MLA_V:(h + 1) * MLA_V, :] = (acc[:MLA_V] / acc[MLA_V:MLA_V + 1]).astype(BF16)
    o_ref[0] = h_ref[0] + _dot_tn(ot_ref[...], wo_ref[...])


def _mla_attn(q, kn, kr, vt, w_out, h3):
    bsz, seq, d = h3.shape
    tq = ATT_TQ
    nh = MLA_HEADS
    full = lambda n: pl.BlockSpec((1, seq, n), lambda b, i: (b, 0, 0))
    return pl.pallas_call(
        _attn_kernel,
        grid=(bsz, seq // tq),
        in_specs=[pl.BlockSpec((1, tq, nh * MLA_QPAD), lambda b, i: (b, i, 0)),
                  full(nh * MLA_NOPE), full(LANES),
                  pl.BlockSpec((1, nh * MLA_VROWS, seq), lambda b, i: (b, 0, 0)),
                  _const_spec(w_out.shape),
                  pl.BlockSpec((1, tq, d), lambda b, i: (b, i, 0))],
        out_specs=pl.BlockSpec((1, tq, d), lambda b, i: (b, i, 0)),
        out_shape=jax.ShapeDtypeStruct((bsz, seq, d), F32),
        scratch_shapes=[pltpu.VMEM((nh * MLA_V, tq), BF16),
                        pltpu.VMEM((nh, 1, tq), F32),
                        pltpu.VMEM((nh, MLA_VROWS, tq), F32),
                        pltpu.VMEM((ATT_TK, tq), F32)],
        compiler_params=_params(2),
        name="mla_attn",
    )(q, kn, kr, vt, w_out, h3)


def _mla_weights(w_in, w_uq, w_ukv):
    nh, r = MLA_HEADS, MLA_ROPE
    qk = MLA_NOPE + MLA_ROPE

    def swap(w):
        return jnp.concatenate([-w[..., r // 2:], w[..., :r // 2]], axis=-1)

    k_rope = w_in[:, MLA_Q_RANK + MLA_KV_RANK:]
    w_in2 = jnp.concatenate([w_in, swap(k_rope)], axis=1)
    wq = w_uq.reshape(MLA_Q_RANK, nh, qk) * (qk ** -0.5 * math.log2(math.e))
    wq2 = jnp.concatenate([wq, swap(wq[..., MLA_NOPE:])], axis=-1).reshape(MLA_Q_RANK, nh * MLA_QPAD)
    wkv = w_ukv.reshape(MLA_KV_RANK, nh, 2, MLA_NOPE)
    w_ukn = wkv[:, :, 0, :].reshape(MLA_KV_RANK, nh * MLA_NOPE)
    w_uvt = jnp.pad(wkv[:, :, 1, :], ((0, 0), (0, 0), (0, MLA_VROWS - MLA_V)))
    w_uvt = w_uvt.reshape(MLA_KV_RANK, nh * MLA_VROWS).T
    return w_in2.astype(BF16), wq2.astype(BF16), w_ukn.astype(BF16), w_uvt.astype(BF16)


def _rope_table(positions):
    r = MLA_ROPE
    freqs = ROPE_THETA ** (-jnp.arange(0, r, 2, dtype=F32) / r)
    freq4 = jnp.tile(freqs, LANES // (r // 2))
    phase = jnp.where(jnp.arange(LANES) < r, 0.0, math.pi / 2).astype(F32)
    return jnp.cos(positions.astype(F32).reshape(-1, 1) * freq4 - phase)


def kernel(x, positions, norm_mix_g, norm_ffn_g, final_norm_g, even_w_in, hgrn_lb_logits, hgrn_norm_g, s5_a_re, s5_a_im, s5_log_dt, s5_b_re, s5_b_im, s5_c_re, s5_c_im, s5_d, s5_w_glu, s5_b_glu, even_w_out, odd_w_in, mla_q_norm_g, mla_w_uq, mla_kv_norm_g, mla_w_ukv, odd_w_out, ffn_w_in, ffn_conv_w, ffn_conv_b, ffn_w_out):
    bsz, seq, d = x.shape
    t = bsz * seq
    depth = norm_mix_g.shape[0]
    lower_bounds = jnp.cumsum(jax.nn.softmax(hgrn_lb_logits.astype(F32), axis=0), axis=0)
    hgrn_dim = HGRN_HEADS * HGRN_HEAD_DIM
    h = x.reshape(t, d)
    ffn_w_in_bf, ffn_w_out_bf = ffn_w_in.astype(BF16), ffn_w_out.astype(BF16)
    for layer in range(depth):
        j = layer // 2
        mix = ()
        if layer % 2 == 0:
            proj = _norm_proj(h, norm_mix_g[layer], even_w_in[j].astype(BF16))
            proj3 = proj.reshape(bsz, seq, -1)
            ya = _hgrn(proj3, lower_bounds[j], hgrn_norm_g[j])
            s5_dim = s5_d.shape[1]
            tables = _s5_tables(s5_a_re[j], s5_a_im[j], s5_log_dt[j], s5_b_re[j], s5_b_im[j],
                                s5_c_re[j], s5_c_im[j], s5_d[j])
            yb = _s5(proj3, (4 * hgrn_dim) // s5_dim, tables, s5_w_glu[j], s5_b_glu[j])
            mix = ((ya.reshape(t, -1), even_w_out[j][:hgrn_dim]),
                   (yb.reshape(t, -1), even_w_out[j][hgrn_dim:]))
        else:
            w_in2, wq2, w_ukn, w_uvt = _mla_weights(odd_w_in[j], mla_w_uq[j], mla_w_ukv[j])
            q, kn, kr, vt = _mla_proj(h, seq, norm_mix_g[layer], w_in2, mla_q_norm_g[j], wq2,
                                      mla_kv_norm_g[j], w_ukn, w_uvt, _rope_table(positions))
            r3 = lambda a: a.reshape(bsz, seq, -1)
            h = _mla_attn(r3(q), r3(kn), r3(kr), vt, odd_w_out[j].astype(BF16),
                          h.reshape(bsz, seq, d)).reshape(t, d)
        last = layer == depth - 1
        h = _ffn(h, seq, layer, norm_ffn_g[layer], ffn_w_in_bf, ffn_conv_w[layer], ffn_conv_b[layer],
                 ffn_w_out_bf, final_norm_g if last else None, mix)
    return h.reshape(bsz, seq, d)
```

```python
import functools
import math

import jax
import jax.numpy as jnp
from jax import lax
from jax.experimental import pallas as pl
from jax.experimental.pallas import tpu as pltpu

F32 = jnp.float32
BF16 = jnp.bfloat16

EPS = 1e-6
HGRN_HEADS = 4
HGRN_HEAD_DIM = 128
HGRN_CHUNK = 64
S5_GROUP = 16
S5_STATE = 64
MLA_HEADS = 8
MLA_Q_RANK = 384
MLA_KV_RANK = 256
MLA_NOPE = 128
MLA_ROPE = 64
MLA_V = 128
ROPE_THETA = 10000.0
LANES = 128
MLA_QPAD = 2 * LANES
BF16_SUBLANES = 16
MLA_VROWS = MLA_V + BF16_SUBLANES

VMEM_LIMIT = 56 * 1024 * 1024


def _params(n_axes, vmem=VMEM_LIMIT):
    return pltpu.CompilerParams(dimension_semantics=("arbitrary",) * n_axes,
                                vmem_limit_bytes=vmem)


def _const_spec(shape):
    zeros = (0,) * len(shape)
    return pl.BlockSpec(shape, lambda *_: zeros, pipeline_mode=pl.Buffered(1))


def _rms(x, g):
    return x * lax.rsqrt(jnp.mean(x * x, axis=-1, keepdims=True) + EPS) * g


def _sigmoid(x):
    return 0.5 * jnp.tanh(0.5 * x) + 0.5


def _dot(a, b):
    return jnp.dot(a, b, preferred_element_type=F32)


def _dot_nt(a, b):
    return lax.dot_general(a, b, (((1,), (1,)), ((), ())), preferred_element_type=F32)


def _dot_tn(a, b):
    return lax.dot_general(a, b, (((0,), (0,)), ((), ())), preferred_element_type=F32)


def _norm_proj_kernel(h_ref, g_ref, w_ref, o_ref):
    hn = _rms(h_ref[...], g_ref[...]).astype(BF16)
    o_ref[...] = _dot(hn, w_ref[...]).astype(o_ref.dtype)


def _norm_proj(h, g, w, tm=1024):
    t, d = h.shape
    n = w.shape[1]
    return pl.pallas_call(
        _norm_proj_kernel,
        grid=(t // tm,),
        in_specs=[pl.BlockSpec((tm, d), lambda i: (i, 0)),
                  _const_spec((1, d)),
                  _const_spec((d, n))],
        out_specs=pl.BlockSpec((tm, n), lambda i: (i, 0)),
        out_shape=jax.ShapeDtypeStruct((t, n), F32),
        compiler_params=_params(1),
        name="even_in",
    )(h, g.reshape(1, d), w)


HGRN_TILE = 1024
HGRN_ATT_BLOCK = 256


def _hgrn_kernel(q_ref, f_ref, i_ref, g_ref, lb_ref, ng_ref, o_ref, st_ref):
    L, C, AB = HGRN_TILE, HGRN_CHUNK, HGRN_ATT_BLOCK
    dh = HGRN_HEAD_DIM

    nh = HGRN_HEADS

    @pl.when(pl.program_id(1) == 0)
    def _():
        st_ref[...] = jnp.zeros_like(st_ref)

    q = q_ref[0]
    f = f_ref[0]
    g = g_ref[0]
    lb = lb_ref[...]
    forget = lb + (1.0 - lb) * _sigmoid(f)
    kh = 1.0 - forget
    ri = lax.broadcasted_iota(jnp.int32, (AB, AB), 0)
    ci = lax.broadcasted_iota(jnp.int32, (AB, AB), 1)
    mask = jnp.logical_and((ri // C) == (ci // C), ci <= ri)

    logf = jnp.log(forget)
    lf_hi = logf.astype(BF16)
    lf_lo = (logf - lf_hi.astype(F32)).astype(BF16)
    tri = jnp.where(mask, 1.0, 0.0).astype(BF16)
    b = jnp.concatenate(
        [_dot(tri, lf_hi[r0:r0 + AB]) + _dot(tri, lf_lo[r0:r0 + AB]) for r0 in range(0, L, AB)],
        axis=0)
    qd = (q * jnp.exp(b)).astype(BF16)
    kd32 = kh * jnp.exp(-b)
    kd = kd32.astype(BF16)
    vb = i_ref[0].astype(BF16)
    nc = L // C
    chunk = [slice(c * C, (c + 1) * C) for c in range(nc)]
    head = [slice(h * dh, (h + 1) * dh) for h in range(nh)]
    bl = [b[c * C + C - 1:(c + 1) * C, :] for c in range(nc)]
    decay = [jnp.exp(x) for x in bl]
    kdec = jnp.concatenate([kd32[chunk[c]] * decay[c] for c in range(nc)], axis=0).astype(BF16)
    ds = [[_dot_tn(vb[chunk[c], head[h]], kdec[chunk[c], head[h]]) for h in range(nh)]
          for c in range(nc)]
    st = [st_ref[h] for h in range(nh)]
    st_in = []
    for c in range(nc):
        st_in.append([s.astype(BF16) for s in st])
        st = [st[h] * decay[c][:, head[h]] + ds[c][h] for h in range(nh)]
    for h in range(nh):
        st_ref[h] = st[h]
    block = [slice(k * AB, (k + 1) * AB) for k in range(L // AB)]
    att = [[_dot_nt(qd[r, head[h]], kd[r, head[h]]) for h in range(nh)] for r in block]
    att = [[jnp.where(mask, a, 0.0).astype(BF16) for a in row] for row in att]
    o_intra = [[_dot(att[k][h], vb[block[k], head[h]]) for h in range(nh)]
               for k in range(len(block))]
    o_inter = [[_dot_nt(qd[chunk[c], head[h]], st_in[c][h]) for h in range(nh)] for c in range(nc)]
    outs = []
    for h in range(nh):
        o = (jnp.concatenate([row[h] for row in o_intra], axis=0)
             + jnp.concatenate([row[h] for row in o_inter], axis=0))
        outs.append(o * lax.rsqrt(jnp.mean(o * o, axis=-1, keepdims=True) + EPS))
    o = jnp.concatenate(outs, axis=1) * ng_ref[...]
    o_ref[0] = (o * (g * _sigmoid(g))).astype(o_ref.dtype)


def _hgrn(proj3, lb, norm_g):
    bsz, seq, _ = proj3.shape
    nh, dh, L = HGRN_HEADS, HGRN_HEAD_DIM, HGRN_TILE
    w = nh * dh

    def col(k):
        return pl.BlockSpec((1, L, w), lambda b, s, k=k: (b, s, k))

    return pl.pallas_call(
        _hgrn_kernel,
        grid=(bsz, seq // L),
        in_specs=[col(0), col(1), col(2), col(3), _const_spec((1, w)), _const_spec((1, w))],
        out_specs=pl.BlockSpec((1, L, w), lambda b, s: (b, s, 0)),
        out_shape=jax.ShapeDtypeStruct((bsz, seq, w), BF16),
        scratch_shapes=[pltpu.VMEM((nh, dh, dh), F32)],
        compiler_params=_params(2),
        name="hgrn2",
    )(proj3, proj3, proj3, proj3, lb.reshape(1, w), norm_g.reshape(1, w))


S5_LC = 16
S5_RC = 16
LANE_BLOCKS = LANES // S5_GROUP


def _s5_tables(a_re, a_im, log_dt, b_re, b_im, c_re, c_im, d_skip):
    G, P = a_re.shape
    Hc = b_re.shape[-1]
    Lc = S5_LC
    dt = jnp.exp(log_dt)[:, None]
    lam, th = a_re * dt, a_im * dt
    mag = jnp.exp(lam)
    abar_re, abar_im = mag * jnp.cos(th), mag * jnp.sin(th)
    den = a_re * a_re + a_im * a_im
    xr, xi = abar_re - 1.0, abar_im
    coef_re = ((xr * a_re + xi * a_im) / den)[..., None]
    coef_im = ((xi * a_re - xr * a_im) / den)[..., None]
    bb_re = coef_re * b_re - coef_im * b_im
    bb_im = coef_re * b_im + coef_im * b_re
    n = jnp.arange(Lc + 1, dtype=F32)[:, None, None]
    pw_re = jnp.exp(n * lam) * jnp.cos(n * th)
    pw_im = jnp.exp(n * lam) * jnp.sin(n * th)
    bt_re, bt_im = bb_re.transpose(0, 2, 1), bb_im.transpose(0, 2, 1)
    cp_re = c_re[None] * pw_re[:Lc, :, None, :] - c_im[None] * pw_im[:Lc, :, None, :]
    cp_im = c_re[None] * pw_im[:Lc, :, None, :] + c_im[None] * pw_re[:Lc, :, None, :]
    kern = jnp.sum(cp_re[:, :, :, None, :] * bt_re[None, :, None, :, :]
                   - cp_im[:, :, :, None, :] * bt_im[None, :, None, :, :], axis=-1)
    skip = d_skip.reshape(G, Hc)[:, :, None] * jnp.eye(Hc, dtype=F32)
    kern = kern.at[0].add(skip)
    kcat = kern.transpose(1, 3, 0, 2).reshape(G, Hc, Lc * Hc)
    pr = pw_re[:Lc][::-1].transpose(1, 0, 2)[:, :, None, :]
    pi = pw_im[:Lc][::-1].transpose(1, 0, 2)[:, :, None, :]
    bs_re = pr * bt_re[:, None] - pi * bt_im[:, None]
    bs_im = pr * bt_im[:, None] + pi * bt_re[:, None]
    bs = jnp.concatenate([bs_re, bs_im], axis=-1).reshape(G, Lc * Hc, 2 * P)
    bs_twin = jnp.concatenate([bs_im, bs_re], axis=-1).reshape(G, Lc * Hc, 2 * P)
    ct_re = c_re.transpose(0, 2, 1)[:, :, None, :]
    ct_im = c_im.transpose(0, 2, 1)[:, :, None, :]
    qr = pw_re[1:].transpose(1, 2, 0)[..., None]
    qi = pw_im[1:].transpose(1, 2, 0)[..., None]
    cs = jnp.concatenate([ct_re * qr - ct_im * qi, -(ct_re * qi + ct_im * qr)],
                         axis=1).reshape(G, 2 * P, Lc * Hc)
    bsw = jnp.concatenate([bs, bs_twin], axis=2).astype(BF16)
    a_n_re, a_n_im = pw_re[Lc], pw_im[Lc]
    m1 = jnp.concatenate([a_n_re, a_n_re], axis=1).reshape(G, 1, 2 * P)
    m2 = jnp.concatenate([-a_n_im, a_n_im], axis=1).reshape(G, 1, 2 * P)
    return kcat, bsw, cs.astype(BF16), m1, m2


def _gelu_tanh(x):
    return 0.5 * x * (1.0 + jnp.tanh(math.sqrt(2.0 / math.pi) * (x + 0.044715 * (x * x * x))))


def _transpose_lane_blocks(vs, blk):
    vs = list(vs)
    d = LANE_BLOCKS // 2
    while d >= 1:
        low = (blk & d) == 0
        for i in range(LANE_BLOCKS):
            if i & d == 0:
                a, b = vs[i], vs[i + d]
                vs[i] = jnp.where(low, a, pltpu.roll(b, d * S5_GROUP, axis=1))
                vs[i + d] = jnp.where(low, pltpu.roll(a, LANES - d * S5_GROUP, axis=1), b)
        d //= 2
    return vs


def _s5_kernel(*refs):
    nq = len(refs) - 15
    u_refs = refs[:nq]
    (kcat_ref, bsw_ref, cs_ref, m1_ref, m2_ref, wglu_ref, bglu_ref, o_ref,
     h_ref, toep_ref, x_ref, y_ref, dh_ref, hs_ref, yt_ref) = refs[nq:]
    nb, tok, _ = u_refs[0].shape
    Lc = S5_LC
    RC = tok // Lc
    R = nb * RC
    G = bsw_ref.shape[0]
    gw = Lc * S5_GROUP
    nhalf = Lc // LANE_BLOCKS

    @pl.when(pl.program_id(0) == 0)
    def _():
        h_ref[...] = jnp.zeros_like(h_ref)
        lane = lax.broadcasted_iota(jnp.int32, (S5_GROUP, gw), 1)

        def build(g, carry):
            kc = kcat_ref[g]
            for s in range(Lc):
                rows = kc if s == 0 else jnp.where(
                    lane >= s * S5_GROUP, pltpu.roll(kc, s * S5_GROUP, axis=1), 0.0)
                toep_ref[g, s * S5_GROUP:(s + 1) * S5_GROUP, :] = rows.astype(BF16)
            return carry

        lax.fori_loop(0, G, build, 0)

    blk = lax.broadcasted_iota(jnp.int32, (R, LANES), 1) // S5_GROUP

    for half in range(nhalf):
        for vq in range(nq):
            ws = _transpose_lane_blocks(
                [u_refs[vq][:, pl.ds(half * LANE_BLOCKS + tl, RC, stride=Lc), :].reshape(R, LANES)
                 for tl in range(LANE_BLOCKS)], blk)
            for gl in range(LANE_BLOCKS):
                col = (nhalf * (LANE_BLOCKS * vq + gl) + half) * LANES
                x_ref[:, col:col + LANES] = ws[gl].astype(BF16)

    sw = 2 * S5_STATE
    for g in range(G):
        xg = x_ref[:, g * gw:(g + 1) * gw]
        y_ref[:, g * gw:(g + 1) * gw] = _dot(xg, toep_ref[g])
        r = _dot(xg, bsw_ref[g])
        dh_ref[0, g] = r[:, :sw]
        dh_ref[1, g] = r[:, sw:]

    hp = [h_ref[0, g] for g in range(G)]
    hq = [h_ref[1, g] for g in range(G)]
    for c in range(RC):
        rows = pl.ds(c, nb, stride=RC)
        for g in range(G):
            hs_ref[g, rows, :] = hp[g]
            m1, m2 = m1_ref[g], m2_ref[g]
            hp[g], hq[g] = (hp[g] * m1 + hq[g] * m2 + dh_ref[0, g, rows, :],
                            hq[g] * m1 - hp[g] * m2 + dh_ref[1, g, rows, :])
    for g in range(G):
        h_ref[0, g] = hp[g]
        h_ref[1, g] = hq[g]

    for g in range(G):
        y_ref[:, g * gw:(g + 1) * gw] += _dot(hs_ref[g].astype(BF16), cs_ref[g])

    for half in range(nhalf):
        for vq in range(nq):
            cols = [(nhalf * (LANE_BLOCKS * vq + gl) + half) * LANES for gl in range(LANE_BLOCKS)]
            ws = _transpose_lane_blocks([y_ref[:, c0:c0 + LANES] for c0 in cols], blk)
            for tl in range(LANE_BLOCKS):
                yt_ref[vq, :, pl.ds(half * LANE_BLOCKS + tl, RC, stride=Lc), :] = (
                    ws[tl].reshape(nb, RC, LANES))

    for b in range(nb):
        z = _gelu_tanh(jnp.concatenate([yt_ref[vq, b] for vq in range(nq)], axis=1))
        gate = _sigmoid(_dot(z.astype(BF16), wglu_ref[...]) + bglu_ref[...])
        o_ref[b] = (z * gate).astype(o_ref.dtype)


def _s5(proj3, u_col_block, tables, w_glu, b_glu):
    bsz, seq, _ = proj3.shape
    kcat, bsw, cs, m1, m2 = tables
    dim = w_glu.shape[0]
    tok = S5_RC * S5_LC
    rows = bsz * S5_RC
    G, _, sw = m1.shape
    gw = kcat.shape[2]
    nq = dim // LANES
    q0 = u_col_block * nq
    u_specs = [pl.BlockSpec((bsz, tok, LANES), lambda i, q=q: (0, i, q0 + q)) for q in range(nq)]
    return pl.pallas_call(
        _s5_kernel,
        grid=(seq // tok,),
        in_specs=u_specs + [_const_spec(kcat.shape), _const_spec(bsw.shape), _const_spec(cs.shape),
                            _const_spec(m1.shape), _const_spec(m2.shape),
                            _const_spec(w_glu.shape), _const_spec((1, dim))],
        out_specs=pl.BlockSpec((bsz, tok, dim), lambda i: (0, i, 0)),
        out_shape=jax.ShapeDtypeStruct((bsz, seq, dim), BF16),
        scratch_shapes=[pltpu.VMEM((2, G, bsz, sw), F32),
                        pltpu.VMEM((G, gw, gw), BF16),
                        pltpu.VMEM((rows, S5_LC * dim), BF16),
                        pltpu.VMEM((rows, S5_LC * dim), F32),
                        pltpu.VMEM((2, G, rows, sw), F32),
                        pltpu.VMEM((G, rows, sw), F32),
                        pltpu.VMEM((nq, bsz, tok, LANES), F32)],
        compiler_params=_params(1),
        name="s5",
    )(*([proj3] * nq), kcat, bsw, cs, m1, m2, w_glu.astype(BF16), b_glu.reshape(1, dim))


FFN_TILE = 1024
FFN_CHUNK = 256
SUBLANES = 8


def _ffn_kernel(tiles_per_seq, final_norm, n_mix, *refs):
    mix = refs[:2 * n_mix]
    (h_ref, g_ref, wa_ref, wu_ref, cw_ref, cb_ref, wo_ref, fg_ref, o_ref,
     prev_ref, abuf_ref, act_ref) = refs[2 * n_mix:]
    tm, fc, hal = FFN_TILE, FFN_CHUNK, SUBLANES
    dff = wa_ref.shape[1]

    @pl.when(pl.program_id(0) % tiles_per_seq == 0)
    def _():
        prev_ref[...] = jnp.zeros_like(prev_ref)

    x = h_ref[...]
    for k in range(n_mix):
        x = x + _dot(mix[2 * k][...], mix[2 * k + 1][...])
    hn = _rms(x, g_ref[...]).astype(BF16)
    for c in range(dff // fc):
        cols = slice(c * fc, (c + 1) * fc)
        a = _dot(hn, wa_ref[:, cols])
        u = _dot(hn, wu_ref[:, cols])
        ab = abuf_ref.at[c % 2]
        ab[0:hal, :] = prev_ref[:, cols]
        ab[hal:hal + tm, :] = a
        prev_ref[:, cols] = a[tm - hal:tm, :]
        a1 = ab[hal - 1:hal - 1 + tm, :]
        a2 = ab[hal - 2:hal - 2 + tm, :]
        w = cw_ref[:, cols]
        conv = a * w[2:3, :] + a1 * w[1:2, :] + a2 * w[0:1, :] + cb_ref[:, cols]
        act_ref[:, cols] = (conv * _sigmoid(conv) * u).astype(BF16)
    y = x + _dot(act_ref[...], wo_ref[...])
    if final_norm:
        y = _rms(y, fg_ref[...])
    o_ref[...] = y


def _ffn(h, seq, layer, g, w_in, conv_w, conv_b, w_out, final_g=None, mix=()):
    t, d = h.shape
    dff = w_out.shape[1]
    tm, fc, hal = FFN_TILE, FFN_CHUNK, SUBLANES
    final_norm = final_g is not None
    fg = final_g if final_norm else g
    mix_specs, mix_args = [], []
    for y, w in mix:
        mix_specs += [pl.BlockSpec((tm, y.shape[1]), lambda i: (i, 0)), _const_spec(w.shape)]
        mix_args += [y, w.astype(BF16)]
    return pl.pallas_call(
        functools.partial(_ffn_kernel, seq // tm, final_norm, len(mix)),
        grid=(t // tm,),
        in_specs=mix_specs + [
                  pl.BlockSpec((tm, d), lambda i: (i, 0)),
                  _const_spec((1, d)),
                  pl.BlockSpec((None, d, dff), lambda i: (layer, 0, 0), pipeline_mode=pl.Buffered(1)),
                  pl.BlockSpec((None, d, dff), lambda i: (layer, 0, 1), pipeline_mode=pl.Buffered(1)),
                  _const_spec((conv_w.shape[0], dff)), _const_spec((1, dff)),
                  pl.BlockSpec((None, dff, d), lambda i: (layer, 0, 0), pipeline_mode=pl.Buffered(1)),
                  _const_spec((1, d))],
        out_specs=pl.BlockSpec((tm, d), lambda i: (i, 0)),
        out_shape=jax.ShapeDtypeStruct((t, d), F32),
        scratch_shapes=[pltpu.VMEM((hal, dff), F32),
                        pltpu.VMEM((2, hal + tm, fc), F32),
                        pltpu.VMEM((tm, dff), BF16)],
        compiler_params=_params(1),
        name="ffn",
    )(*mix_args, h, g.reshape(1, d), w_in, w_in, conv_w, conv_b.reshape(1, dff),
      w_out, fg.reshape(1, d))


MLA_TILE = 1024


def _mla_proj_kernel(h_ref, g_ref, win_ref, qg_ref, wuq_ref, kvg_ref, wukn_ref, wuvt_ref,
                     rope_ref, q_ref, kn_ref, kr_ref, vt_ref):
    nh = MLA_HEADS
    hn = _rms(h_ref[...], g_ref[...]).astype(BF16)
    proj = _dot(hn, win_ref[...])
    cq = _rms(proj[:, :MLA_Q_RANK], qg_ref[...]).astype(BF16)
    ckv = _rms(proj[:, MLA_Q_RANK:MLA_Q_RANK + MLA_KV_RANK], kvg_ref[...]).astype(BF16)
    tab = rope_ref[...]

    def rope(x):
        y = x * tab
        return y + pltpu.roll(y, MLA_ROPE, axis=1)

    lane = lax.broadcasted_iota(jnp.int32, tab.shape, 1)
    kr = jnp.where(lane < MLA_ROPE, rope(proj[:, MLA_Q_RANK + MLA_KV_RANK:]), 0.0)
    kr_ref[...] = kr.astype(kr_ref.dtype)
    qf = _dot(cq, wuq_ref[...])
    pieces = []
    for h in range(nh):
        lo = h * MLA_QPAD
        pieces.append(qf[:, lo:lo + LANES])
        pieces.append(rope(qf[:, lo + LANES:lo + MLA_QPAD]))
    q_ref[...] = jnp.concatenate(pieces, axis=1).astype(q_ref.dtype)
    kn_ref[...] = _dot(ckv, wukn_ref[...]).astype(kn_ref.dtype)
    vt = _dot_nt(wuvt_ref[...], ckv)
    row = lax.broadcasted_iota(jnp.int32, vt.shape, 0)
    vt_ref[0] = jnp.where(row % MLA_VROWS >= MLA_V, 1.0, vt).astype(vt_ref.dtype)


def _mla_proj(h, seq, g, w_in, q_norm_g, w_uq, kv_norm_g, w_ukn, w_uvt, rope_tab):
    t, d = h.shape
    tm, nh = MLA_TILE, MLA_HEADS
    tps = seq // tm
    row = lambda n: pl.BlockSpec((tm, n), lambda i: (i, 0))
    return pl.pallas_call(
        _mla_proj_kernel,
        grid=(t // tm,),
        in_specs=[row(d), _const_spec((1, d)), _const_spec(w_in.shape),
                  _const_spec((1, MLA_Q_RANK)), _const_spec(w_uq.shape),
                  _const_spec((1, MLA_KV_RANK)), _const_spec(w_ukn.shape), _const_spec(w_uvt.shape),
                  row(LANES)],
        out_specs=[row(nh * MLA_QPAD), row(nh * MLA_NOPE), row(LANES),
                   pl.BlockSpec((1, nh * MLA_VROWS, tm), lambda i: (i // tps, 0, i % tps))],
        out_shape=[jax.ShapeDtypeStruct((t, nh * MLA_QPAD), BF16),
                   jax.ShapeDtypeStruct((t, nh * MLA_NOPE), BF16),
                   jax.ShapeDtypeStruct((t, LANES), BF16),
                   jax.ShapeDtypeStruct((t // seq, nh * MLA_VROWS, seq), BF16)],
        compiler_params=_params(1),
        name="mla_proj",
    )(h, g.reshape(1, d), w_in, q_norm_g.reshape(1, -1), w_uq, kv_norm_g.reshape(1, -1),
      w_ukn, w_uvt, rope_tab)


ATT_TQ = 512
ATT_TK = 512
ATT_LOOKAHEAD = 0
ATT_DIAG_LOOKAHEAD = 2


def _attn_kernel(q_ref, kn_ref, kr_ref, vt_ref, wo_ref, h_ref, o_ref,
                 ot_ref, m_ref, acc_ref, st0_ref):
    tq, tk = ATT_TQ, ATT_TK
    half = tk // 2
    qi = pl.program_id(1)
    nh = MLA_HEADS
    m_ref[...] = jnp.full(m_ref.shape, -jnp.inf, F32)
    acc_ref[...] = jnp.zeros(acc_ref.shape, F32)

    def scores(off, nk, h, qs=slice(None)):
        rows = pl.ds(off, nk)
        kj = jnp.concatenate([kn_ref[0, rows, h * MLA_NOPE:(h + 1) * MLA_NOPE], kr_ref[0, rows, :]],
                             axis=1)
        return _dot_nt(kj, q_ref[0, qs, h * MLA_QPAD:(h + 1) * MLA_QPAD])

    def shifted_logits(h, st, qs=slice(None)):
        m = m_ref[h, :, qs]
        m_new = jnp.maximum(m, jnp.max(st, axis=0, keepdims=True))
        m_ref[h, :, qs] = m_new
        return jnp.exp2(m - m_new), (st - m_new).astype(BF16)

    def accumulate(off, nk, h, alpha, d, qs=slice(None)):
        acc_ref[h, :, qs] = alpha * acc_ref[h, :, qs] + _dot(
            vt_ref[0, h * MLA_VROWS:(h + 1) * MLA_VROWS, pl.ds(off, nk)], jnp.exp2(d))

    def key_tile(off, off_next):
        pending = {0: st0_ref[...]}
        todo = list(range(1, nh)) + [None]

        def issue():
            if todo:
                h = todo.pop(0)
                if h is None:
                    st0_ref[...] = scores(off_next, tk, 0)
                else:
                    pending[h] = scores(off, tk, h)

        for _ in range(ATT_LOOKAHEAD):
            issue()
        for h in range(nh):
            alpha, d = shifted_logits(h, pending.pop(h))
            issue()
            accumulate(off, tk, h, alpha, d)

    def diag_tile(off):
        causal = (lax.broadcasted_iota(jnp.int32, (half, tq), 0)
                  <= lax.broadcasted_iota(jnp.int32, (half, tq), 1))
        upper = slice(half, tq)
        units = []
        for h in range(nh):
            units.append((h, off, slice(None), causal))
            units.append((h, off + half, upper, causal[:, :half]))
        pending = {0: st0_ref[:half, :], 1: st0_ref[half:, half:]}
        todo = list(range(2, len(units)))

        def issue():
            if todo:
                u = todo.pop(0)
                h, koff, qs, _ = units[u]
                pending[u] = scores(koff, half, h, qs)

        for _ in range(ATT_DIAG_LOOKAHEAD):
            issue()
        for u, (h, koff, qs, mask) in enumerate(units):
            alpha, d = shifted_logits(h, jnp.where(mask, pending.pop(u), -jnp.inf), qs)
            issue()
            accumulate(koff, half, h, alpha, d, qs)

    st0_ref[...] = scores(0, tk, 0)

    def body(j, carry):
        key_tile(pl.multiple_of(j * tk, tk), pl.multiple_of((j + 1) * tk, tk))
        return carry

    lax.fori_loop(0, qi, body, 0)
    diag_tile(pl.multiple_of(qi * tk, tk))
    for h in range(nh):
        acc = acc_ref[h]
        ot_ref[h * MLA_V:(h + 1) * MLA_V, :] = (acc[:MLA_V] / acc[MLA_V:MLA_V + 1]).astype(BF16)
    o_ref[0] = h_ref[0] + _dot_tn(ot_ref[...], wo_ref[...])


def _mla_attn(q, kn, kr, vt, w_out, h3):
    bsz, seq, d = h3.shape
    tq = ATT_TQ
    nh = MLA_HEADS
    full = lambda n: pl.BlockSpec((1, seq, n), lambda b, i: (b, 0, 0))
    return pl.pallas_call(
        _attn_kernel,
        grid=(bsz, seq // tq),
        in_specs=[pl.BlockSpec((1, tq, nh * MLA_QPAD), lambda b, i: (b, i, 0)),
                  full(nh * MLA_NOPE), full(LANES),
                  pl.BlockSpec((1, nh * MLA_VROWS, seq), lambda b, i: (b, 0, 0)),
                  _const_spec(w_out.shape),
                  pl.BlockSpec((1, tq, d), lambda b, i: (b, i, 0))],
        out_specs=pl.BlockSpec((1, tq, d), lambda b, i: (b, i, 0)),
        out_shape=jax.ShapeDtypeStruct((bsz, seq, d), F32),
        scratch_shapes=[pltpu.VMEM((nh * MLA_V, tq), BF16),
                        pltpu.VMEM((nh, 1, tq), F32),
                        pltpu.VMEM((nh, MLA_VROWS, tq), F32),
                        pltpu.VMEM((ATT_TK, tq), F32)],
        compiler_params=_params(2),
        name="mla_attn",
    )(q, kn, kr, vt, w_out, h3)


def _mla_weights(w_in, w_uq, w_ukv):
    nh, r = MLA_HEADS, MLA_ROPE
    qk = MLA_NOPE + MLA_ROPE

    def swap(w):
        return jnp.concatenate([-w[..., r // 2:], w[..., :r // 2]], axis=-1)

    k_rope = w_in[:, MLA_Q_RANK + MLA_KV_RANK:]
    w_in2 = jnp.concatenate([w_in, swap(k_rope)], axis=1)
    wq = w_uq.reshape(MLA_Q_RANK, nh, qk) * (qk ** -0.5 * math.log2(math.e))
    wq2 = jnp.concatenate([wq, swap(wq[..., MLA_NOPE:])], axis=-1).reshape(MLA_Q_RANK, nh * MLA_QPAD)
    wkv = w_ukv.reshape(MLA_KV_RANK, nh, 2, MLA_NOPE)
    w_ukn = wkv[:, :, 0, :].reshape(MLA_KV_RANK, nh * MLA_NOPE)
    w_uvt = jnp.pad(wkv[:, :, 1, :], ((0, 0), (0, 0), (0, MLA_VROWS - MLA_V)))
    w_uvt = w_uvt.reshape(MLA_KV_RANK, nh * MLA_VROWS).T
    return w_in2.astype(BF16), wq2.astype(BF16), w_ukn.astype(BF16), w_uvt.astype(BF16)


def _rope_table(positions):
    r = MLA_ROPE
    freqs = ROPE_THETA ** (-jnp.arange(0, r, 2, dtype=F32) / r)
    freq4 = jnp.tile(freqs, LANES // (r // 2))
    phase = jnp.where(jnp.arange(LANES) < r, 0.0, math.pi / 2).astype(F32)
    return jnp.cos(positions.astype(F32).reshape(-1, 1) * freq4 - phase)


def kernel(x, positions, norm_mix_g, norm_ffn_g, final_norm_g, even_w_in, hgrn_lb_logits, hgrn_norm_g, s5_a_re, s5_a_im, s5_log_dt, s5_b_re, s5_b_im, s5_c_re, s5_c_im, s5_d, s5_w_glu, s5_b_glu, even_w_out, odd_w_in, mla_q_norm_g, mla_w_uq, mla_kv_norm_g, mla_w_ukv, odd_w_out, ffn_w_in, ffn_conv_w, ffn_conv_b, ffn_w_out):
    bsz, seq, d = x.shape
    t = bsz * seq
    depth = norm_mix_g.shape[0]
    lower_bounds = jnp.cumsum(jax.nn.softmax(hgrn_lb_logits.astype(F32), axis=0), axis=0)
    hgrn_dim = HGRN_HEADS * HGRN_HEAD_DIM
    h = x.reshape(t, d)
    ffn_w_in_bf, ffn_w_out_bf = ffn_w_in.astype(BF16), ffn_w_out.astype(BF16)
    for layer in range(depth):
        j = layer // 2
        mix = ()
        if layer % 2 == 0:
            proj = _norm_proj(h, norm_mix_g[layer], even_w_in[j].astype(BF16))
            proj3 = proj.reshape(bsz, seq, -1)
            ya = _hgrn(proj3, lower_bounds[j], hgrn_norm_g[j])
            s5_dim = s5_d.shape[1]
            tables = _s5_tables(s5_a_re[j], s5_a_im[j], s5_log_dt[j], s5_b_re[j], s5_b_im[j],
                                s5_c_re[j], s5_c_im[j], s5_d[j])
            yb = _s5(proj3, (4 * hgrn_dim) // s5_dim, tables, s5_w_glu[j], s5_b_glu[j])
            mix = ((ya.reshape(t, -1), even_w_out[j][:hgrn_dim]),
                   (yb.reshape(t, -1), even_w_out[j][hgrn_dim:]))
        else:
            w_in2, wq2, w_ukn, w_uvt = _mla_weights(odd_w_in[j], mla_w_uq[j], mla_w_ukv[j])
            q, kn, kr, vt = _mla_proj(h, seq, norm_mix_g[layer], w_in2, mla_q_norm_g[j], wq2,
                                      mla_kv_norm_g[j], w_ukn, w_uvt, _rope_table(positions))
            r3 = lambda a: a.reshape(bsz, seq, -1)
            h = _mla_attn(r3(q), r3(kn), r3(kr), vt, odd_w_out[j].astype(BF16),
                          h.reshape(bsz, seq, d)).reshape(t, d)
        last = layer == depth - 1
        h = _ffn(h, seq, layer, norm_ffn_g[layer], ffn_w_in_bf, ffn_conv_w[layer], ffn_conv_b[layer],
                 ffn_w_out_bf, final_norm_g if last else None, mix)
    return h.reshape(bsz, seq, d)
```

```python
import functools
import math

import jax
import jax.numpy as jnp
from jax import lax
from jax.experimental import pallas as pl
from jax.experimental.pallas import tpu as pltpu

F32 = jnp.float32
BF16 = jnp.bfloat16

EPS = 1e-6
HGRN_HEADS = 4
HGRN_HEAD_DIM = 128
HGRN_CHUNK = 64
S5_GROUP = 16
S5_STATE = 64
MLA_HEADS = 8
MLA_Q_RANK = 384
MLA_KV_RANK = 256
MLA_NOPE = 128
MLA_ROPE = 64
MLA_V = 128
ROPE_THETA = 10000.0
LANES = 128
MLA_QPAD = 2 * LANES
BF16_SUBLANES = 16
MLA_VROWS = MLA_V + BF16_SUBLANES

VMEM_LIMIT = 56 * 1024 * 1024


def _params(n_axes, vmem=VMEM_LIMIT):
    return pltpu.CompilerParams(dimension_semantics=("arbitrary",) * n_axes,
                                vmem_limit_bytes=vmem)


def _const_spec(shape):
    zeros = (0,) * len(shape)
    return pl.BlockSpec(shape, lambda *_: zeros, pipeline_mode=pl.Buffered(1))


def _rms(x, g):
    return x * lax.rsqrt(jnp.mean(x * x, axis=-1, keepdims=True) + EPS) * g


def _sigmoid(x):
    return 0.5 * jnp.tanh(0.5 * x) + 0.5


def _dot(a, b):
    return jnp.dot(a, b, preferred_element_type=F32)


def _dot_nt(a, b):
    return lax.dot_general(a, b, (((1,), (1,)), ((), ())), preferred_element_type=F32)


def _dot_tn(a, b):
    return lax.dot_general(a, b, (((0,), (0,)), ((), ())), preferred_element_type=F32)


def _norm_proj_kernel(h_ref, g_ref, w_ref, o_ref):
    hn = _rms(h_ref[...], g_ref[...]).astype(BF16)
    o_ref[...] = _dot(hn, w_ref[...]).astype(o_ref.dtype)


def _norm_proj(h, g, w, tm=1024):
    t, d = h.shape
    n = w.shape[1]
    return pl.pallas_call(
        _norm_proj_kernel,
        grid=(t // tm,),
        in_specs=[pl.BlockSpec((tm, d), lambda i: (i, 0)),
                  _const_spec((1, d)),
                  _const_spec((d, n))],
        out_specs=pl.BlockSpec((tm, n), lambda i: (i, 0)),
        out_shape=jax.ShapeDtypeStruct((t, n), F32),
        compiler_params=_params(1),
        name="even_in",
    )(h, g.reshape(1, d), w)


HGRN_TILE = 1024
HGRN_ATT_BLOCK = 256


def _hgrn_kernel(q_ref, f_ref, i_ref, g_ref, lb_ref, ng_ref, o_ref, st_ref):
    L, C, AB = HGRN_TILE, HGRN_CHUNK, HGRN_ATT_BLOCK
    dh = HGRN_HEAD_DIM

    nh = HGRN_HEADS

    @pl.when(pl.program_id(1) == 0)
    def _():
        st_ref[...] = jnp.zeros_like(st_ref)

    q = q_ref[0]
    f = f_ref[0]
    g = g_ref[0]
    lb = lb_ref[...]
    forget = lb + (1.0 - lb) * _sigmoid(f)
    kh = 1.0 - forget
    ri = lax.broadcasted_iota(jnp.int32, (AB, AB), 0)
    ci = lax.broadcasted_iota(jnp.int32, (AB, AB), 1)
    mask = jnp.logical_and((ri // C) == (ci // C), ci <= ri)

    logf = jnp.log(forget)
    lf_hi = logf.astype(BF16)
    lf_lo = (logf - lf_hi.astype(F32)).astype(BF16)
    tri = jnp.where(mask, 1.0, 0.0).astype(BF16)
    b = jnp.concatenate(
        [_dot(tri, lf_hi[r0:r0 + AB]) + _dot(tri, lf_lo[r0:r0 + AB]) for r0 in range(0, L, AB)],
        axis=0)
    qd = (q * jnp.exp(b)).astype(BF16)
    kd32 = kh * jnp.exp(-b)
    kd = kd32.astype(BF16)
    vb = i_ref[0].astype(BF16)
    nc = L // C
    chunk = [slice(c * C, (c + 1) * C) for c in range(nc)]
    head = [slice(h * dh, (h + 1) * dh) for h in range(nh)]
    bl = [b[c * C + C - 1:(c + 1) * C, :] for c in range(nc)]
    decay = [jnp.exp(x) for x in bl]
    kdec = jnp.concatenate([kd32[chunk[c]] * decay[c] for c in range(nc)], axis=0).astype(BF16)
    ds = [[_dot_tn(vb[chunk[c], head[h]], kdec[chunk[c], head[h]]) for h in range(nh)]
          for c in range(nc)]
    st = [st_ref[h] for h in range(nh)]
    st_in = []
    for c in range(nc):
        st_in.append([s.astype(BF16) for s in st])
        st = [st[h] * decay[c][:, head[h]] + ds[c][h] for h in range(nh)]
    for h in range(nh):
        st_ref[h] = st[h]
    block = [slice(k * AB, (k + 1) * AB) for k in range(L // AB)]
    att = [[_dot_nt(qd[r, head[h]], kd[r, head[h]]) for h in range(nh)] for r in block]
    att = [[jnp.where(mask, a, 0.0).astype(BF16) for a in row] for row in att]
    o_intra = [[_dot(att[k][h], vb[block[k], head[h]]) for h in range(nh)]
               for k in range(len(block))]
    o_inter = [[_dot_nt(qd[chunk[c], head[h]], st_in[c][h]) for h in range(nh)] for c in range(nc)]
    outs = []
    for h in range(nh):
        o = (jnp.concatenate([row[h] for row in o_intra], axis=0)
             + jnp.concatenate([row[h] for row in o_inter], axis=0))
        outs.append(o * lax.rsqrt(jnp.mean(o * o, axis=-1, keepdims=True) + EPS))
    o = jnp.concatenate(outs, axis=1) * ng_ref[...]
    o_ref[0] = (o * (g * _sigmoid(g))).astype(o_ref.dtype)


def _hgrn(proj3, lb, norm_g):
    bsz, seq, _ = proj3.shape
    nh, dh, L = HGRN_HEADS, HGRN_HEAD_DIM, HGRN_TILE
    w = nh * dh

    def col(k):
        return pl.BlockSpec((1, L, w), lambda b, s, k=k: (b, s, k))

    return pl.pallas_call(
        _hgrn_kernel,
        grid=(bsz, seq // L),
        in_specs=[col(0), col(1), col(2), col(3), _const_spec((1, w)), _const_spec((1, w))],
        out_specs=pl.BlockSpec((1, L, w), lambda b, s: (b, s, 0)),
        out_shape=jax.ShapeDtypeStruct((bsz, seq, w), BF16),
        scratch_shapes=[pltpu.VMEM((nh, dh, dh), F32)],
        compiler_params=_params(2),
        name="hgrn2",
    )(proj3, proj3, proj3, proj3, lb.reshape(1, w), norm_g.reshape(1, w))


S5_LC = 16
S5_RC = 16
LANE_BLOCKS = LANES // S5_GROUP


def _s5_tables(a_re, a_im, log_dt, b_re, b_im, c_re, c_im, d_skip):
    G, P = a_re.shape
    Hc = b_re.shape[-1]
    Lc = S5_LC
    dt = jnp.exp(log_dt)[:, None]
    lam, th = a_re * dt, a_im * dt
    mag = jnp.exp(lam)
    abar_re, abar_im = mag * jnp.cos(th), mag * jnp.sin(th)
    den = a_re * a_re + a_im * a_im
    xr, xi = abar_re - 1.0, abar_im
    coef_re = ((xr * a_re + xi * a_im) / den)[..., None]
    coef_im = ((xi * a_re - xr * a_im) / den)[..., None]
    bb_re = coef_re * b_re - coef_im * b_im
    bb_im = coef_re * b_im + coef_im * b_re
    n = jnp.arange(Lc + 1, dtype=F32)[:, None, None]
    pw_re = jnp.exp(n * lam) * jnp.cos(n * th)
    pw_im = jnp.exp(n * lam) * jnp.sin(n * th)
    bt_re, bt_im = bb_re.transpose(0, 2, 1), bb_im.transpose(0, 2, 1)
    cp_re = c_re[None] * pw_re[:Lc, :, None, :] - c_im[None] * pw_im[:Lc, :, None, :]
    cp_im = c_re[None] * pw_im[:Lc, :, None, :] + c_im[None] * pw_re[:Lc, :, None, :]
    kern = jnp.sum(cp_re[:, :, :, None, :] * bt_re[None, :, None, :, :]
                   - cp_im[:, :, :, None, :] * bt_im[None, :, None, :, :], axis=-1)
    skip = d_skip.reshape(G, Hc)[:, :, None] * jnp.eye(Hc, dtype=F32)
    kern = kern.at[0].add(skip)
    kcat = kern.transpose(1, 3, 0, 2).reshape(G, Hc, Lc * Hc)
    pr = pw_re[:Lc][::-1].transpose(1, 0, 2)[:, :, None, :]
    pi = pw_im[:Lc][::-1].transpose(1, 0, 2)[:, :, None, :]
    bs_re = pr * bt_re[:, None] - pi * bt_im[:, None]
    bs_im = pr * bt_im[:, None] + pi * bt_re[:, None]
    bs = jnp.concatenate([bs_re, bs_im], axis=-1).reshape(G, Lc * Hc, 2 * P)
    bs_twin = jnp.concatenate([bs_im, bs_re], axis=-1).reshape(G, Lc * Hc, 2 * P)
    ct_re = c_re.transpose(0, 2, 1)[:, :, None, :]
    ct_im = c_im.transpose(0, 2, 1)[:, :, None, :]
    qr = pw_re[1:].transpose(1, 2, 0)[..., None]
    qi = pw_im[1:].transpose(1, 2, 0)[..., None]
    cs = jnp.concatenate([ct_re * qr - ct_im * qi, -(ct_re * qi + ct_im * qr)],
                         axis=1).reshape(G, 2 * P, Lc * Hc)
    bsw = jnp.concatenate([bs, bs_twin], axis=2).astype(BF16)
    a_n_re, a_n_im = pw_re[Lc], pw_im[Lc]
    m1 = jnp.concatenate([a_n_re, a_n_re], axis=1).reshape(G, 1, 2 * P)
    m2 = jnp.concatenate([-a_n_im, a_n_im], axis=1).reshape(G, 1, 2 * P)
    return kcat, bsw, cs.astype(BF16), m1, m2


def _gelu_tanh(x):
    return 0.5 * x * (1.0 + jnp.tanh(math.sqrt(2.0 / math.pi) * (x + 0.044715 * (x * x * x))))


def _transpose_lane_blocks(vs, blk):
    vs = list(vs)
    d = LANE_BLOCKS // 2
    while d >= 1:
        low = (blk & d) == 0
        for i in range(LANE_BLOCKS):
            if i & d == 0:
                a, b = vs[i], vs[i + d]
                vs[i] = jnp.where(low, a, pltpu.roll(b, d * S5_GROUP, axis=1))
                vs[i + d] = jnp.where(low, pltpu.roll(a, LANES - d * S5_GROUP, axis=1), b)
        d //= 2
    return vs


def _s5_kernel(*refs):
    nq = len(refs) - 15
    u_refs = refs[:nq]
    (kcat_ref, bsw_ref, cs_ref, m1_ref, m2_ref, wglu_ref, bglu_ref, o_ref,
     h_ref, toep_ref, x_ref, y_ref, dh_ref, hs_ref, yt_ref) = refs[nq:]
    nb, tok, _ = u_refs[0].shape
    Lc = S5_LC
    RC = tok // Lc
    R = nb * RC
    G = bsw_ref.shape[0]
    gw = Lc * S5_GROUP
    nhalf = Lc // LANE_BLOCKS

    @pl.when(pl.program_id(0) == 0)
    def _():
        h_ref[...] = jnp.zeros_like(h_ref)
        lane = lax.broadcasted_iota(jnp.int32, (S5_GROUP, gw), 1)

        def build(g, carry):
            kc = kcat_ref[g]
            for s in range(Lc):
                rows = kc if s == 0 else jnp.where(
                    lane >= s * S5_GROUP, pltpu.roll(kc, s * S5_GROUP, axis=1), 0.0)
                toep_ref[g, s * S5_GROUP:(s + 1) * S5_GROUP, :] = rows.astype(BF16)
            return carry

        lax.fori_loop(0, G, build, 0)

    blk = lax.broadcasted_iota(jnp.int32, (R, LANES), 1) // S5_GROUP

    for half in range(nhalf):
        for vq in range(nq):
            ws = _transpose_lane_blocks(
                [u_refs[vq][:, pl.ds(half * LANE_BLOCKS + tl, RC, stride=Lc), :].reshape(R, LANES)
                 for tl in range(LANE_BLOCKS)], blk)
            for gl in range(LANE_BLOCKS):
                col = (nhalf * (LANE_BLOCKS * vq + gl) + half) * LANES
                x_ref[:, col:col + LANES] = ws[gl].astype(BF16)

    sw = 2 * S5_STATE
    for g in range(G):
        xg = x_ref[:, g * gw:(g + 1) * gw]
        y_ref[:, g * gw:(g + 1) * gw] = _dot(xg, toep_ref[g])
        r = _dot(xg, bsw_ref[g])
        dh_ref[0, g] = r[:, :sw]
        dh_ref[1, g] = r[:, sw:]

    hp = [h_ref[0, g] for g in range(G)]
    hq = [h_ref[1, g] for g in range(G)]
    for c in range(RC):
        rows = pl.ds(c, nb, stride=RC)
        for g in range(G):
            hs_ref[g, rows, :] = hp[g]
            m1, m2 = m1_ref[g], m2_ref[g]
            hp[g], hq[g] = (hp[g] * m1 + hq[g] * m2 + dh_ref[0, g, rows, :],
                            hq[g] * m1 - hp[g] * m2 + dh_ref[1, g, rows, :])
    for g in range(G):
        h_ref[0, g] = hp[g]
        h_ref[1, g] = hq[g]

    for g in range(G):
        y_ref[:, g * gw:(g + 1) * gw] += _dot(hs_ref[g].astype(BF16), cs_ref[g])

    for half in range(nhalf):
        for vq in range(nq):
            cols = [(nhalf * (LANE_BLOCKS * vq + gl) + half) * LANES for gl in range(LANE_BLOCKS)]
            ws = _transpose_lane_blocks([y_ref[:, c0:c0 + LANES] for c0 in cols], blk)
            for tl in range(LANE_BLOCKS):
                yt_ref[vq, :, pl.ds(half * LANE_BLOCKS + tl, RC, stride=Lc), :] = (
                    ws[tl].reshape(nb, RC, LANES))

    for b in range(nb):
        z = _gelu_tanh(jnp.concatenate([yt_ref[vq, b] for vq in range(nq)], axis=1))
        gate = _sigmoid(_dot(z.astype(BF16), wglu_ref[...]) + bglu_ref[...])
        o_ref[b] = (z * gate).astype(o_ref.dtype)


def _s5(proj3, u_col_block, tables, w_glu, b_glu):
    bsz, seq, _ = proj3.shape
    kcat, bsw, cs, m1, m2 = tables
    dim = w_glu.shape[0]
    tok = S5_RC * S5_LC
    rows = bsz * S5_RC
    G, _, sw = m1.shape
    gw = kcat.shape[2]
    nq = dim // LANES
    q0 = u_col_block * nq
    u_specs = [pl.BlockSpec((bsz, tok, LANES), lambda i, q=q: (0, i, q0 + q)) for q in range(nq)]
    return pl.pallas_call(
        _s5_kernel,
        grid=(seq // tok,),
        in_specs=u_specs + [_const_spec(kcat.shape), _const_spec(bsw.shape), _const_spec(cs.shape),
                            _const_spec(m1.shape), _const_spec(m2.shape),
                            _const_spec(w_glu.shape), _const_spec((1, dim))],
        out_specs=pl.BlockSpec((bsz, tok, dim), lambda i: (0, i, 0)),
        out_shape=jax.ShapeDtypeStruct((bsz, seq, dim), BF16),
        scratch_shapes=[pltpu.VMEM((2, G, bsz, sw), F32),
                        pltpu.VMEM((G, gw, gw), BF16),
                        pltpu.VMEM((rows, S5_LC * dim), BF16),
                        pltpu.VMEM((rows, S5_LC * dim), F32),
                        pltpu.VMEM((2, G, rows, sw), F32),
                        pltpu.VMEM((G, rows, sw), F32),
                        pltpu.VMEM((nq, bsz, tok, LANES), F32)],
        compiler_params=_params(1),
        name="s5",
    )(*([proj3] * nq), kcat, bsw, cs, m1, m2, w_glu.astype(BF16), b_glu.reshape(1, dim))


FFN_TILE = 1024
FFN_CHUNK = 256
SUBLANES = 8


def _ffn_kernel(tiles_per_seq, final_norm, n_mix, *refs):
    mix = refs[:2 * n_mix]
    (h_ref, g_ref, wa_ref, wu_ref, cw_ref, cb_ref, wo_ref, fg_ref, o_ref,
     prev_ref, abuf_ref, act_ref) = refs[2 * n_mix:]
    tm, fc, hal = FFN_TILE, FFN_CHUNK, SUBLANES
    dff = wa_ref.shape[1]

    @pl.when(pl.program_id(0) % tiles_per_seq == 0)
    def _():
        prev_ref[...] = jnp.zeros_like(prev_ref)

    x = h_ref[...]
    for k in range(n_mix):
        x = x + _dot(mix[2 * k][...], mix[2 * k + 1][...])
    hn = _rms(x, g_ref[...]).astype(BF16)
    for c in range(dff // fc):
        cols = slice(c * fc, (c + 1) * fc)
        a = _dot(hn, wa_ref[:, cols])
        u = _dot(hn, wu_ref[:, cols])
        ab = abuf_ref.at[c % 2]
        ab[0:hal, :] = prev_ref[:, cols]
        ab[hal:hal + tm, :] = a
        prev_ref[:, cols] = a[tm - hal:tm, :]
        a1 = ab[hal - 1:hal - 1 + tm, :]
        a2 = ab[hal - 2:hal - 2 + tm, :]
        w = cw_ref[:, cols]
        conv = a * w[2:3, :] + a1 * w[1:2, :] + a2 * w[0:1, :] + cb_ref[:, cols]
        act_ref[:, cols] = (conv * _sigmoid(conv) * u).astype(BF16)
    y = x + _dot(act_ref[...], wo_ref[...])
    if final_norm:
        y = _rms(y, fg_ref[...])
    o_ref[...] = y


def _ffn(h, seq, layer, g, w_in, conv_w, conv_b, w_out, final_g=None, mix=()):
    t, d = h.shape
    dff = w_out.shape[1]
    tm, fc, hal = FFN_TILE, FFN_CHUNK, SUBLANES
    final_norm = final_g is not None
    fg = final_g if final_norm else g
    mix_specs, mix_args = [], []
    for y, w in mix:
        mix_specs += [pl.BlockSpec((tm, y.shape[1]), lambda i: (i, 0)), _const_spec(w.shape)]
        mix_args += [y, w.astype(BF16)]
    return pl.pallas_call(
        functools.partial(_ffn_kernel, seq // tm, final_norm, len(mix)),
        grid=(t // tm,),
        in_specs=mix_specs + [
                  pl.BlockSpec((tm, d), lambda i: (i, 0)),
                  _const_spec((1, d)),
                  pl.BlockSpec((None, d, dff), lambda i: (layer, 0, 0), pipeline_mode=pl.Buffered(1)),
                  pl.BlockSpec((None, d, dff), lambda i: (layer, 0, 1), pipeline_mode=pl.Buffered(1)),
                  _const_spec((conv_w.shape[0], dff)), _const_spec((1, dff)),
                  pl.BlockSpec((None, dff, d), lambda i: (layer, 0, 0), pipeline_mode=pl.Buffered(1)),
                  _const_spec((1, d))],
        out_specs=pl.BlockSpec((tm, d), lambda i: (i, 0)),
        out_shape=jax.ShapeDtypeStruct((t, d), F32),
        scratch_shapes=[pltpu.VMEM((hal, dff), F32),
                        pltpu.VMEM((2, hal + tm, fc), F32),
                        pltpu.VMEM((tm, dff), BF16)],
        compiler_params=_params(1),
        name="ffn",
    )(*mix_args, h, g.reshape(1, d), w_in, w_in, conv_w, conv_b.reshape(1, dff),
      w_out, fg.reshape(1, d))


MLA_TILE = 1024


def _mla_proj_kernel(h_ref, g_ref, win_ref, qg_ref, wuq_ref, kvg_ref, wukn_ref, wuvt_ref,
                     rope_ref, q_ref, kn_ref, kr_ref, vt_ref):
    nh = MLA_HEADS
    hn = _rms(h_ref[...], g_ref[...]).astype(BF16)
    proj = _dot(hn, win_ref[...])
    cq = _rms(proj[:, :MLA_Q_RANK], qg_ref[...]).astype(BF16)
    ckv = _rms(proj[:, MLA_Q_RANK:MLA_Q_RANK + MLA_KV_RANK], kvg_ref[...]).astype(BF16)
    tab = rope_ref[...]

    def rope(x):
        y = x * tab
        return y + pltpu.roll(y, MLA_ROPE, axis=1)

    lane = lax.broadcasted_iota(jnp.int32, tab.shape, 1)
    kr = jnp.where(lane < MLA_ROPE, rope(proj[:, MLA_Q_RANK + MLA_KV_RANK:]), 0.0)
    kr_ref[...] = kr.astype(kr_ref.dtype)
    qf = _dot(cq, wuq_ref[...])
    pieces = []
    for h in range(nh):
        lo = h * MLA_QPAD
        pieces.append(qf[:, lo:lo + LANES])
        pieces.append(rope(qf[:, lo + LANES:lo + MLA_QPAD]))
    q_ref[...] = jnp.concatenate(pieces, axis=1).astype(q_ref.dtype)
    kn_ref[...] = _dot(ckv, wukn_ref[...]).astype(kn_ref.dtype)
    vt = _dot_nt(wuvt_ref[...], ckv)
    row = lax.broadcasted_iota(jnp.int32, vt.shape, 0)
    vt_ref[0] = jnp.where(row % MLA_VROWS >= MLA_V, 1.0, vt).astype(vt_ref.dtype)


def _mla_proj(h, seq, g, w_in, q_norm_g, w_uq, kv_norm_g, w_ukn, w_uvt, rope_tab):
    t, d = h.shape
    tm, nh = MLA_TILE, MLA_HEADS
    tps = seq // tm
    row = lambda n: pl.BlockSpec((tm, n), lambda i: (i, 0))
    return pl.pallas_call(
        _mla_proj_kernel,
        grid=(t // tm,),
        in_specs=[row(d), _const_spec((1, d)), _const_spec(w_in.shape),
                  _const_spec((1, MLA_Q_RANK)), _const_spec(w_uq.shape),
                  _const_spec((1, MLA_KV_RANK)), _const_spec(w_ukn.shape), _const_spec(w_uvt.shape),
                  row(LANES)],
        out_specs=[row(nh * MLA_QPAD), row(nh * MLA_NOPE), row(LANES),
                   pl.BlockSpec((1, nh * MLA_VROWS, tm), lambda i: (i // tps, 0, i % tps))],
        out_shape=[jax.ShapeDtypeStruct((t, nh * MLA_QPAD), BF16),
                   jax.ShapeDtypeStruct((t, nh * MLA_NOPE), BF16),
                   jax.ShapeDtypeStruct((t, LANES), BF16),
                   jax.ShapeDtypeStruct((t // seq, nh * MLA_VROWS, seq), BF16)],
        compiler_params=_params(1),
        name="mla_proj",
    )(h, g.reshape(1, d), w_in, q_norm_g.reshape(1, -1), w_uq, kv_norm_g.reshape(1, -1),
      w_ukn, w_uvt, rope_tab)


ATT_TQ = 512
ATT_TK = 512
ATT_LOOKAHEAD = 0
ATT_DIAG_LOOKAHEAD = 2


def _attn_kernel(q_ref, kn_ref, kr_ref, vt_ref, wo_ref, h_ref, o_ref,
                 ot_ref, m_ref, acc_ref, st0_ref):
    tq, tk = ATT_TQ, ATT_TK
    half = tk // 2
    qi = pl.program_id(1)
    nh = MLA_HEADS
    m_ref[...] = jnp.full(m_ref.shape, -jnp.inf, F32)
    acc_ref[...] = jnp.zeros(acc_ref.shape, F32)

    def scores(off, nk, h, qs=slice(None)):
        rows = pl.ds(off, nk)
        kj = jnp.concatenate([kn_ref[0, rows, h * MLA_NOPE:(h + 1) * MLA_NOPE], kr_ref[0, rows, :]],
                             axis=1)
        return _dot_nt(kj, q_ref[0, qs, h * MLA_QPAD:(h + 1) * MLA_QPAD])

    def shifted_logits(h, st, qs=slice(None)):
        m = m_ref[h, :, qs]
        w = st.shape[1] // 2
        parts = []
        for c in (slice(0, w), slice(w, 2 * w)):
            m_c = jnp.maximum(m[:, c], jnp.max(st[:, c], axis=0, keepdims=True))
            parts.append((m_c, (st[:, c] - m_c).astype(BF16)))
        m_new = jnp.concatenate([p[0] for p in parts], axis=1)
        m_ref[h, :, qs] = m_new
        return jnp.exp2(m - m_new), jnp.concatenate([p[1] for p in parts], axis=1)

    def accumulate(off, nk, h, alpha, d, qs=slice(None)):
        acc_ref[h, :, qs] = alpha * acc_ref[h, :, qs] + _dot(
            vt_ref[0, h * MLA_VROWS:(h + 1) * MLA_VROWS, pl.ds(off, nk)], jnp.exp2(d))

    def key_tile(off, off_next):
        pending = {0: st0_ref[...]}
        todo = list(range(1, nh)) + [None]

        def issue():
            if todo:
                h = todo.pop(0)
                if h is None:
                    st0_ref[...] = scores(off_next, tk, 0)
                else:
                    pending[h] = scores(off, tk, h)

        for _ in range(ATT_LOOKAHEAD):
            issue()
        for h in range(nh):
            alpha, d = shifted_logits(h, pending.pop(h))
            issue()
            accumulate(off, tk, h, alpha, d)

    def diag_tile(off):
        causal = (lax.broadcasted_iota(jnp.int32, (half, tq), 0)
                  <= lax.broadcasted_iota(jnp.int32, (half, tq), 1))
        upper = slice(half, tq)
        units = []
        for h in range(nh):
            units.append((h, off, slice(None), causal))
            units.append((h, off + half, upper, causal[:, :half]))
        pending = {0: st0_ref[:half, :], 1: st0_ref[half:, half:]}
        todo = list(range(2, len(units)))

        def issue():
            if todo:
                u = todo.pop(0)
                h, koff, qs, _ = units[u]
                pending[u] = scores(koff, half, h, qs)

        for _ in range(ATT_DIAG_LOOKAHEAD):
            issue()
        for u, (h, koff, qs, mask) in enumerate(units):
            alpha, d = shifted_logits(h, jnp.where(mask, pending.pop(u), -jnp.inf), qs)
            issue()
            accumulate(koff, half, h, alpha, d, qs)

    st0_ref[...] = scores(0, tk, 0)

    def body(j, carry):
        key_tile(pl.multiple_of(j * tk, tk), pl.multiple_of((j + 1) * tk, tk))
        return carry

    lax.fori_loop(0, qi, body, 0)
    diag_tile(pl.multiple_of(qi * tk, tk))
    for h in range(nh):
        acc = acc_ref[h]
        ot_ref[h * MLA_V:(h + 1) * MLA_V, :] = (acc[:MLA_V] / acc[MLA_V:MLA_V + 1]).astype(BF16)
    o_ref[0] = h_ref[0] + _dot_tn(ot_ref[...], wo_ref[...])


def _mla_attn(q, kn, kr, vt, w_out, h3):
    bsz, seq, d = h3.shape
    tq = ATT_TQ
    nh = MLA_HEADS
    full = lambda n: pl.BlockSpec((1, seq, n), lambda b, i: (b, 0, 0))
    return pl.pallas_call(
        _attn_kernel,
        grid=(bsz, seq // tq),
        in_specs=[pl.BlockSpec((1, tq, nh * MLA_QPAD), lambda b, i: (b, i, 0)),
                  full(nh * MLA_NOPE), full(LANES),
                  pl.BlockSpec((1, nh * MLA_VROWS, seq), lambda b, i: (b, 0, 0)),
                  _const_spec(w_out.shape),
                  pl.BlockSpec((1, tq, d), lambda b, i: (b, i, 0))],
        out_specs=pl.BlockSpec((1, tq, d), lambda b, i: (b, i, 0)),
        out_shape=jax.ShapeDtypeStruct((bsz, seq, d), F32),
        scratch_shapes=[pltpu.VMEM((nh * MLA_V, tq), BF16),
                        pltpu.VMEM((nh, 1, tq), F32),
                        pltpu.VMEM((nh, MLA_VROWS, tq), F32),
                        pltpu.VMEM((ATT_TK, tq), F32)],
        compiler_params=_params(2),
        name="mla_attn",
    )(q, kn, kr, vt, w_out, h3)


def _mla_weights(w_in, w_uq, w_ukv):
    nh, r = MLA_HEADS, MLA_ROPE
    qk = MLA_NOPE + MLA_ROPE

    def swap(w):
        return jnp.concatenate([-w[..., r // 2:], w[..., :r // 2]], axis=-1)

    k_rope = w_in[:, MLA_Q_RANK + MLA_KV_RANK:]
    w_in2 = jnp.concatenate([w_in, swap(k_rope)], axis=1)
    wq = w_uq.reshape(MLA_Q_RANK, nh, qk) * (qk ** -0.5 * math.log2(math.e))
    wq2 = jnp.concatenate([wq, swap(wq[..., MLA_NOPE:])], axis=-1).reshape(MLA_Q_RANK, nh * MLA_QPAD)
    wkv = w_ukv.reshape(MLA_KV_RANK, nh, 2, MLA_NOPE)
    w_ukn = wkv[:, :, 0, :].reshape(MLA_KV_RANK, nh * MLA_NOPE)
    w_uvt = jnp.pad(wkv[:, :, 1, :], ((0, 0), (0, 0), (0, MLA_VROWS - MLA_V)))
    w_uvt = w_uvt.reshape(MLA_KV_RANK, nh * MLA_VROWS).T
    return w_in2.astype(BF16), wq2.astype(BF16), w_ukn.astype(BF16), w_uvt.astype(BF16)


def _rope_table(positions):
    r = MLA_ROPE
    freqs = ROPE_THETA ** (-jnp.arange(0, r, 2, dtype=F32) / r)
    freq4 = jnp.tile(freqs, LANES // (r // 2))
    phase = jnp.where(jnp.arange(LANES) < r, 0.0, math.pi / 2).astype(F32)
    return jnp.cos(positions.astype(F32).reshape(-1, 1) * freq4 - phase)


def kernel(x, positions, norm_mix_g, norm_ffn_g, final_norm_g, even_w_in, hgrn_lb_logits, hgrn_norm_g, s5_a_re, s5_a_im, s5_log_dt, s5_b_re, s5_b_im, s5_c_re, s5_c_im, s5_d, s5_w_glu, s5_b_glu, even_w_out, odd_w_in, mla_q_norm_g, mla_w_uq, mla_kv_norm_g, mla_w_ukv, odd_w_out, ffn_w_in, ffn_conv_w, ffn_conv_b, ffn_w_out):
    bsz, seq, d = x.shape
    t = bsz * seq
    depth = norm_mix_g.shape[0]
    lower_bounds = jnp.cumsum(jax.nn.softmax(hgrn_lb_logits.astype(F32), axis=0), axis=0)
    hgrn_dim = HGRN_HEADS * HGRN_HEAD_DIM
    h = x.reshape(t, d)
    ffn_w_in_bf, ffn_w_out_bf = ffn_w_in.astype(BF16), ffn_w_out.astype(BF16)
    for layer in range(depth):
        j = layer // 2
        mix = ()
        if layer % 2 == 0:
            proj = _norm_proj(h, norm_mix_g[layer], even_w_in[j].astype(BF16))
            proj3 = proj.reshape(bsz, seq, -1)
            ya = _hgrn(proj3, lower_bounds[j], hgrn_norm_g[j])
            s5_dim = s5_d.shape[1]
            tables = _s5_tables(s5_a_re[j], s5_a_im[j], s5_log_dt[j], s5_b_re[j], s5_b_im[j],
                                s5_c_re[j], s5_c_im[j], s5_d[j])
            yb = _s5(proj3, (4 * hgrn_dim) // s5_dim, tables, s5_w_glu[j], s5_b_glu[j])
            mix = ((ya.reshape(t, -1), even_w_out[j][:hgrn_dim]),
                   (yb.reshape(t, -1), even_w_out[j][hgrn_dim:]))
        else:
            w_in2, wq2, w_ukn, w_uvt = _mla_weights(odd_w_in[j], mla_w_uq[j], mla_w_ukv[j])
            q, kn, kr, vt = _mla_proj(h, seq, norm_mix_g[layer], w_in2, mla_q_norm_g[j], wq2,
                                      mla_kv_norm_g[j], w_ukn, w_uvt, _rope_table(positions))
            r3 = lambda a: a.reshape(bsz, seq, -1)
            h = _mla_attn(r3(q), r3(kn), r3(kr), vt, odd_w_out[j].astype(BF16),
                          h.reshape(bsz, seq, d)).reshape(t, d)
        last = layer == depth - 1
        h = _ffn(h, seq, layer, norm_ffn_g[layer], ffn_w_in_bf, ffn_conv_w[layer], ffn_conv_b[layer],
                 ffn_w_out_bf, final_norm_g if last else None, mix)
    return h.reshape(bsz, seq, d)
```
